```python
import jax, jax.numpy as jnp
from jax import lax
import numpy as np

D_MODEL = 2048
BATCH = 4
SEQ = 4096
DEPTH = 2

GRID_W = 64
CTX_LEN = 256
HEAD_DIM = 128
A_Q_HEADS = 6
A_KV_HEADS = 2
B_Q_HEADS = 6
B_KV_HEADS = 2
CONV_CH = D_MODEL - (A_Q_HEADS + B_Q_HEADS) * HEAD_DIM
CONV_WIDTH = 3
WINDOW = 128
Q_BLOCK = 128
ROPE_BASE = 10000.0
ROPE_PAIRS = HEAD_DIM // 4
NORM_EPS = 1e-6
NEG_INF = -1e30
ATTN_SCALE = HEAD_DIM ** -0.5
FFN_HIDDEN = ((8 * D_MODEL + 3 * 256 - 1) // (3 * 256)) * 256

KV_W_A = A_KV_HEADS * HEAD_DIM
KV_W_B = B_KV_HEADS * HEAD_DIM
OFF_AK = 0
OFF_AV = OFF_AK + KV_W_A
OFF_BK = OFF_AV + KV_W_A
OFF_BV = OFF_BK + KV_W_B
KV_COLS = OFF_BV + KV_W_B
OFF_AQ = KV_COLS
OFF_BQ = OFF_AQ + A_Q_HEADS * HEAD_DIM
OFF_CONV = OFF_BQ + B_Q_HEADS * HEAD_DIM
IN_COLS = OFF_CONV + 3 * CONV_CH
N_MOD = 6

kernel_name = "hybrid_parallel_groups_dit_block"


def rms_norm(x, g):
    xf = x.astype(jnp.float32)
    y = xf * lax.rsqrt(jnp.mean(xf * xf, axis=-1, keepdims=True) + NORM_EPS)
    return (y * g.astype(jnp.float32)).astype(x.dtype)


def cols(t, start, width):
    return t[..., start:start + width]


def split_heads(t, n_heads):
    return t.reshape(*t.shape[:-1], n_heads, HEAD_DIM)


def axial_rope_tables(rows):
    row = jnp.repeat(jnp.arange(rows, dtype=jnp.int32), GRID_W)
    col = jnp.tile(jnp.arange(GRID_W, dtype=jnp.int32), rows)
    inv_freq = jnp.power(ROPE_BASE, -jnp.arange(ROPE_PAIRS, dtype=jnp.float32) / ROPE_PAIRS)
    ang = jnp.stack([row, col], axis=1).astype(jnp.float32)[:, :, None] * inv_freq
    return jnp.cos(ang), jnp.sin(ang)


def apply_axial_rope(x, cos, sin):
    xs = x.astype(jnp.float32).reshape(*x.shape[:-1], 2, 2, ROPE_PAIRS)
    x1, x2 = xs[..., 0, :], xs[..., 1, :]
    c, s = cos[:, None], sin[:, None]
    out = jnp.stack([x1 * c - x2 * s, x2 * c + x1 * s], axis=-2)
    return out.reshape(x.shape).astype(x.dtype)


def dense_gqa(q, k, v, sink=None):
    b, lq, h, dh = q.shape
    hkv = k.shape[2]
    g = h // hkv
    qg = q.reshape(b, lq, hkv, g, dh)
    s = jnp.einsum('bqhgd,bkhd->bhgqk', qg, k).astype(jnp.float32) * ATTN_SCALE
    if sink is not None:
        sink_col = jnp.broadcast_to(sink.astype(jnp.float32).reshape(hkv, g, 1, 1), s.shape[:-1] + (1,))
        p = jax.nn.softmax(jnp.concatenate([s, sink_col], axis=-1), axis=-1)[..., :-1]
    else:
        p = jax.nn.softmax(s, axis=-1)
    o = jnp.einsum('bhgqk,bkhd->bqhgd', p.astype(v.dtype), v)
    return o.reshape(b, lq, h * dh)


def global_attention_latent(q, k, v, kc, vc):
    b, s, h, dh = q.shape
    nblk = s // Q_BLOCK
    keys = jnp.concatenate([kc, k], axis=1)
    vals = jnp.concatenate([vc, v], axis=1)
    qb = q.reshape(b, nblk, Q_BLOCK, h, dh).transpose(1, 0, 2, 3, 4)
    o = lax.map(lambda qblk: dense_gqa(qblk, keys, vals), qb)
    return o.transpose(1, 0, 2, 3).reshape(b, s, h * dh)


def window_attention_latent(q, k, v, kc, vc, sink):
    b, s, h, dh = q.shape
    hkv = k.shape[2]
    g = h // hkv
    nblk = s // Q_BLOCK

    def band(t):
        tp = jnp.pad(t, ((0, 0), (Q_BLOCK, Q_BLOCK), (0, 0), (0, 0))).reshape(b, nblk + 2, Q_BLOCK, hkv, dh)
        return jnp.concatenate([tp[:, :-2], tp[:, 1:-1], tp[:, 2:]], axis=2)

    kb, vb = band(k), band(v)
    qg = q.reshape(b, nblk, Q_BLOCK, hkv, g, dh)
    s_win = jnp.einsum('bnqhgd,bnkhd->bnhgqk', qg, kb).astype(jnp.float32) * ATTN_SCALE
    blk = jnp.arange(nblk, dtype=jnp.int32)[:, None] * Q_BLOCK
    qpos = blk + jnp.arange(Q_BLOCK, dtype=jnp.int32)[None]
    kpos = blk - Q_BLOCK + jnp.arange(3 * Q_BLOCK, dtype=jnp.int32)[None]
    valid = ((jnp.abs(qpos[:, :, None] - kpos[:, None, :]) <= WINDOW)
             & (kpos[:, None, :] >= 0) & (kpos[:, None, :] < s))
    s_win = jnp.where(valid[None, :, None, None], s_win, NEG_INF)
    s_ctx = jnp.einsum('bnqhgd,bchd->bnhgqc', qg, kc).astype(jnp.float32) * ATTN_SCALE
    sink_col = jnp.broadcast_to(sink.astype(jnp.float32).reshape(1, 1, hkv, g, 1, 1), s_ctx.shape[:-1] + (1,))
    p = jax.nn.softmax(jnp.concatenate([s_win, s_ctx, sink_col], axis=-1), axis=-1)
    n_win = 3 * Q_BLOCK
    n_ctx = kc.shape[1]
    p_win = p[..., :n_win].astype(v.dtype)
    p_ctx = p[..., n_win:n_win + n_ctx].astype(v.dtype)
    o = (jnp.einsum('bnhgqk,bnkhd->bnqhgd', p_win, vb)
         + jnp.einsum('bnhgqc,bchd->bnqhgd', p_ctx, vc))
    return o.reshape(b, s, h * dh)


def short_conv(u, conv_w):
    g_pre, g_post, val = jnp.split(u, 3, axis=-1)
    z = g_pre * val
    w = conv_w.astype(z.dtype)[:, None, :]
    y = lax.conv_general_dilated(z, w, window_strides=(1,),
                                 padding=[(CONV_WIDTH // 2, CONV_WIDTH // 2)],
                                 dimension_numbers=('NWC', 'WIO', 'NWC'),
                                 feature_group_count=CONV_CH)
    return g_post * y


def swiglu(h, w_gate, w_up, w_down):
    return (jax.nn.silu(h @ w_gate) * (h @ w_up)) @ w_down


def hybrid_layer(x, xc, mod, mod_c, g_mix_pre, g_mix_post, g_ffn_pre, g_ffn_post, w_in, q_norm, k_norm,
                 sink, conv_w, w_out, w_gate, w_up, w_down, cos, sin, update_ctx):
    sh1, sc1, gt1, sh2, sc2, gt2 = jnp.split(mod, N_MOD, axis=-1)
    csh1, csc1, cgt1, csh2, csc2, cgt2 = jnp.split(mod_c, N_MOD, axis=-1)

    h = rms_norm(x, g_mix_pre) * (1 + sc1) + sh1
    hc = rms_norm(xc, g_mix_pre) * (1 + csc1) + csh1
    p = h @ w_in
    pc = hc @ w_in if update_ctx else hc @ w_in[:, :KV_COLS]

    kac = rms_norm(split_heads(cols(pc, OFF_AK, KV_W_A), A_KV_HEADS), k_norm)
    vac = split_heads(cols(pc, OFF_AV, KV_W_A), A_KV_HEADS)
    kbc = split_heads(cols(pc, OFF_BK, KV_W_B), B_KV_HEADS)
    vbc = split_heads(cols(pc, OFF_BV, KV_W_B), B_KV_HEADS)

    qa = apply_axial_rope(rms_norm(split_heads(cols(p, OFF_AQ, A_Q_HEADS * HEAD_DIM), A_Q_HEADS), q_norm), cos, sin)
    ka = apply_axial_rope(rms_norm(split_heads(cols(p, OFF_AK, KV_W_A), A_KV_HEADS), k_norm), cos, sin)
    va = split_heads(cols(p, OFF_AV, KV_W_A), A_KV_HEADS)
    oa = global_attention_latent(qa, ka, va, kac, vac)

    qb = apply_axial_rope(split_heads(cols(p, OFF_BQ, B_Q_HEADS * HEAD_DIM), B_Q_HEADS), cos, sin)
    kb = apply_axial_rope(split_heads(cols(p, OFF_BK, KV_W_B), B_KV_HEADS), cos, sin)
    vb = split_heads(cols(p, OFF_BV, KV_W_B), B_KV_HEADS)
    ob = window_attention_latent(qb, kb, vb, kbc, vbc, sink)

    oc = short_conv(cols(p, OFF_CONV, 3 * CONV_CH), conv_w)

    y = jnp.concatenate([oa, ob, oc], axis=-1) @ w_out
    x = x + gt1 * rms_norm(y, g_mix_post)
    h2 = rms_norm(x, g_ffn_pre) * (1 + sc2) + sh2
    x = x + gt2 * rms_norm(swiglu(h2, w_gate, w_up, w_down), g_ffn_post)

    if update_ctx:
        qac = rms_norm(split_heads(cols(pc, OFF_AQ, A_Q_HEADS * HEAD_DIM), A_Q_HEADS), q_norm)
        qbc = split_heads(cols(pc, OFF_BQ, B_Q_HEADS * HEAD_DIM), B_Q_HEADS)
        oac = dense_gqa(qac, kac, vac)
        obc = dense_gqa(qbc, kbc, vbc, sink)
        occ = short_conv(cols(pc, OFF_CONV, 3 * CONV_CH), conv_w)
        yc = jnp.concatenate([oac, obc, occ], axis=-1) @ w_out
        xc = xc + cgt1 * rms_norm(yc, g_mix_post)
        h2c = rms_norm(xc, g_ffn_pre) * (1 + csc2) + csh2
        xc = xc + cgt2 * rms_norm(swiglu(h2c, w_gate, w_up, w_down), g_ffn_post)
    return x, xc


def setup_inputs(seed: int = 0) -> dict:
    key = jax.random.key(seed)
    ks = jax.random.split(key, 20)
    f32 = jnp.float32
    D, F, L = D_MODEL, FFN_HIDDEN, DEPTH

    def nrm(k, shape, scale):
        return jax.random.normal(k, shape, f32) * scale

    return {
        "x": nrm(ks[0], (BATCH, SEQ, D), 1.0),
        "c": nrm(ks[1], (BATCH, D), 1.0),
        "ctx": nrm(ks[2], (BATCH, CTX_LEN, D), 1.0),
        "c_ctx": nrm(ks[3], (D,), 1.0),
        "w_ada": nrm(ks[4], (L, D, N_MOD * D), D ** -0.5),
        "b_ada": nrm(ks[5], (L, N_MOD * D), 0.01),
        "g_mix_pre": 1.0 + nrm(ks[6], (L, D), 0.02),
        "g_mix_post": 1.0 + nrm(ks[7], (L, D), 0.02),
        "g_ffn_pre": 1.0 + nrm(ks[8], (L, D), 0.02),
        "g_ffn_post": 1.0 + nrm(ks[9], (L, D), 0.02),
        "w_in": nrm(ks[10], (L, D, IN_COLS), D ** -0.5),
        "q_norm": 1.0 + nrm(ks[11], (L, HEAD_DIM), 0.02),
        "k_norm": 1.0 + nrm(ks[12], (L, HEAD_DIM), 0.02),
        "sink": nrm(ks[13], (L, B_Q_HEADS), 1.0),
        "conv_w": nrm(ks[14], (L, CONV_WIDTH, CONV_CH), CONV_WIDTH ** -0.5),
        "w_out": nrm(ks[15], (L, D, D), D ** -0.5),
        "w_gate": nrm(ks[16], (L, D, F), D ** -0.5),
        "w_up": nrm(ks[17], (L, D, F), D ** -0.5),
        "w_down": nrm(ks[18], (L, F, D), F ** -0.5),
    }


def reference(x, c, ctx, c_ctx, w_ada, b_ada, g_mix_pre, g_mix_post, g_ffn_pre, g_ffn_post, w_in,
              q_norm, k_norm, sink, conv_w, w_out, w_gate, w_up, w_down):
    rows = x.shape[1] // GRID_W
    cos, sin = axial_rope_tables(rows)
    xc = ctx
    silu_c = jax.nn.silu(c)
    silu_cc = jax.nn.silu(c_ctx)
    for l in range(DEPTH):
        mod = (silu_c @ w_ada[l] + b_ada[l])[:, None, :]
        mod_c = silu_cc @ w_ada[l] + b_ada[l]
        x, xc = hybrid_layer(x, xc, mod, mod_c, g_mix_pre[l], g_mix_post[l], g_ffn_pre[l], g_ffn_post[l],
                             w_in[l], q_norm[l], k_norm[l], sink[l], conv_w[l], w_out[l],
                             w_gate[l], w_up[l], w_down[l], cos, sin, update_ctx=(l < DEPTH - 1))
    return x
```

```python
import functools

import jax
import jax.numpy as jnp
from jax import lax
from jax.experimental import pallas as pl
from jax.experimental.pallas import tpu as pltpu

F32 = jnp.float32
BF16 = jnp.bfloat16

D_MODEL = 2048
HEAD_DIM = 128
GRID_W = 64
A_Q_HEADS = 6
A_KV_HEADS = 2
B_Q_HEADS = 6
B_KV_HEADS = 2
GROUP = A_Q_HEADS // A_KV_HEADS
Q_COLS = A_Q_HEADS * HEAD_DIM
CONV_CH = D_MODEL - 2 * Q_COLS
WINDOW = 128
Q_BLOCK = 128
ROPE_BASE = 10000.0
ROPE_PAIRS = HEAD_DIM // 4
NORM_EPS = 1e-6
NEG_INF = -1e30
ATTN_SCALE = HEAD_DIM ** -0.5
N_MOD = 6
MOD_ROWS = 8

KV_COLS = 4 * A_KV_HEADS * HEAD_DIM
OFF_AQ = KV_COLS
OFF_BQ = OFF_AQ + Q_COLS
OFF_CONV = OFF_BQ + Q_COLS
IN_COLS = OFF_CONV + 3 * CONV_CH

V7X_VMEM_BYTES = 64 * 1024 * 1024
VMEM_LIMIT = 56 * 1024 * 1024
BF16_SUBLANES = 16


def _rms(t):
    return t * lax.rsqrt(jnp.mean(t * t, axis=-1, keepdims=True) + NORM_EPS)


def _silu(t):
    return t * (1.0 / (1.0 + jnp.exp(-t)))


def _dot(a, b):
    return jnp.dot(a, b, preferred_element_type=F32)


def _dot_nt(a, b):
    return lax.dot_general(a, b, (((1,), (1,)), ((), ())), preferred_element_type=F32)


def _params(semantics):
    return pltpu.CompilerParams(dimension_semantics=semantics, vmem_limit_bytes=VMEM_LIMIT)


def _resident(shape, index_map):
    return pl.BlockSpec(shape, index_map, pipeline_mode=pl.Buffered(1))


def _ada_kernel(s_ref, w_ref, b_ref, o_ref):
    a = _silu(s_ref[...]).astype(BF16)
    o_ref[0] = _dot(a, w_ref[0].astype(BF16)) + b_ref[0]


def _ada(cond, w_ada, b_ada, tn=1024):
    depth, d, n = w_ada.shape
    return pl.pallas_call(
        _ada_kernel,
        out_shape=jax.ShapeDtypeStruct((depth, MOD_ROWS, n), F32),
        grid=(depth, n // tn),
        in_specs=[
            pl.BlockSpec((MOD_ROWS, d), lambda l, j: (0, 0)),
            pl.BlockSpec((1, d, tn), lambda l, j: (l, 0, j)),
            pl.BlockSpec((1, 1, tn), lambda l, j: (l, 0, j)),
        ],
        out_specs=pl.BlockSpec((1, MOD_ROWS, tn), lambda l, j: (l, 0, j)),
        compiler_params=_params(("parallel", "parallel")),
        name="ada",
    )(cond, w_ada, b_ada.reshape(depth, 1, n))


def _mod_spec(chunk, row_of_block):
    return pl.BlockSpec((None, 1, D_MODEL), lambda i: (row_of_block(i) * N_MOD + chunk, 0, 0))


def _vec_spec(n):
    return pl.BlockSpec((1, n), lambda i: (0, 0))


def _inproj_kernel(rope, x_ref, g_ref, sc_ref, sh_ref, w_ref, qn_ref, kn_ref, *rest):
    if rope:
        cos_ref, sin_ref, kv_ref, qa_ref, qb_ref, z_ref, gp_ref = rest
        cos = cos_ref[...]
        sin = sin_ref[...]
        lane = lax.broadcasted_iota(jnp.int32, cos.shape, 1)
        first_half = (lane & ROPE_PAIRS) == 0

        def rot(t):
            partner = jnp.where(first_half, pltpu.roll(t, HEAD_DIM - ROPE_PAIRS, 1), pltpu.roll(t, ROPE_PAIRS, 1))
            return t * cos + partner * sin
    else:
        kv_ref, qa_ref, qb_ref, z_ref, gp_ref = rest

        def rot(t):
            return t

    h = _rms(x_ref[...]) * g_ref[...]
    hb = (h * (1.0 + sc_ref[...]) + sh_ref[...]).astype(BF16)

    def proj(c0, n):
        return _dot(hb, w_ref[:, c0:c0 + n])

    def head(t, k):
        return t[:, k * HEAD_DIM:(k + 1) * HEAD_DIM]

    def put(ref, k, t):
        ref[:, k * HEAD_DIM:(k + 1) * HEAD_DIM] = t.astype(BF16)

    pkv = proj(0, KV_COLS)
    k_gain = kn_ref[...] * ATTN_SCALE
    for k in range(A_KV_HEADS):
        put(kv_ref, k, rot(_rms(head(pkv, k)) * k_gain))
        put(kv_ref, A_KV_HEADS + k, head(pkv, A_KV_HEADS + k))
        put(kv_ref, 2 * A_KV_HEADS + k, rot(head(pkv, 2 * A_KV_HEADS + k)) * ATTN_SCALE)
        put(kv_ref, 3 * A_KV_HEADS + k, head(pkv, 3 * A_KV_HEADS + k))

    pq = proj(OFF_AQ, Q_COLS)
    q_gain = qn_ref[...]
    for k in range(A_Q_HEADS):
        put(qa_ref, k, rot(_rms(head(pq, k)) * q_gain))
    pq = proj(OFF_BQ, Q_COLS)
    for k in range(B_Q_HEADS):
        put(qb_ref, k, rot(head(pq, k)))

    z_ref[...] = (proj(OFF_CONV, CONV_CH) * proj(OFF_CONV + 2 * CONV_CH, CONV_CH)).astype(BF16)
    gp_ref[...] = proj(OFF_CONV + CONV_CH, CONV_CH).astype(BF16)


def _inproj(x, mod, row_of_block, g_pre, w_in, q_norm, k_norm, rope_tables, tm, seq_len):
    m = x.shape[0]
    rope = rope_tables is not None
    in_specs = [
        pl.BlockSpec((tm, D_MODEL), lambda i: (i, 0)),
        _vec_spec(D_MODEL),
        _mod_spec(1, row_of_block),
        _mod_spec(0, row_of_block),
        _resident((D_MODEL, IN_COLS), lambda i: (0, 0)),
        _vec_spec(HEAD_DIM),
        _vec_spec(HEAD_DIM),
    ]
    args = [x, g_pre, mod, mod, w_in, q_norm, k_norm]
    if rope:
        blocks_per_seq = seq_len // tm
        in_specs += [pl.BlockSpec((tm, HEAD_DIM), lambda i: (i % blocks_per_seq, 0))] * 2
        args += list(rope_tables)
    widths = (KV_COLS, Q_COLS, Q_COLS, CONV_CH, CONV_CH)
    return pl.pallas_call(
        functools.partial(_inproj_kernel, rope),
        out_shape=[jax.ShapeDtypeStruct((m, w), BF16) for w in widths],
        grid=(m // tm,),
        in_specs=in_specs,
        out_specs=[pl.BlockSpec((tm, w), lambda i: (i, 0)) for w in widths],
        compiler_params=_params(("parallel",)),
        name="inproj_rope" if rope else "inproj",
    )(*args)


def _stack_heads(q_ref, kvh):
    base = kvh * GROUP * HEAD_DIM
    return jnp.concatenate(
        [q_ref[:, base + g * HEAD_DIM: base + (g + 1) * HEAD_DIM] for g in range(GROUP)], axis=0)


def _unstack_heads(o_ref, kvh, o, tq):
    base = kvh * GROUP * HEAD_DIM
    for g in range(GROUP):
        o_ref[:, base + g * HEAD_DIM: base + (g + 1) * HEAD_DIM] = o[g * tq:(g + 1) * tq].astype(BF16)


def _sink_column(sink_ref, kvh, tq):
    grp = lax.broadcasted_iota(jnp.int32, (GROUP * tq, 1), 0) // tq
    col = jnp.full((GROUP * tq, 1), sink_ref[kvh * GROUP], F32)
    for g in range(1, GROUP):
        col = jnp.where(grp == g, sink_ref[kvh * GROUP + g], col)
    return col


def _softmax_pv(scores, values, sink_col=None):
    m = scores[0].max(axis=-1, keepdims=True)
    for s in scores[1:]:
        m = jnp.maximum(m, s.max(axis=-1, keepdims=True))
    if sink_col is not None:
        m = jnp.maximum(m, sink_col)
    den = None
    acc = None
    for s, v in zip(scores, values):
        p = jnp.exp(s - m)
        d = p.sum(axis=-1, keepdims=True)
        a = _dot(p.astype(BF16), v)
        den = d if den is None else den + d
        acc = a if acc is None else acc + a
    if sink_col is not None:
        den = den + jnp.exp(sink_col - m)
    return acc / den


def _attn_a_kernel(tq, tk, q_ref, kc_ref, vc_ref, k_ref, v_ref, o_ref, m_scr, l_scr, acc_scr):
    q = _stack_heads(q_ref, 0)
    s = _dot_nt(q, kc_ref[...])
    m = s.max(axis=-1, keepdims=True)
    p = jnp.exp(s - m)
    m_scr[...] = m
    l_scr[...] = p.sum(axis=-1, keepdims=True)
    acc_scr[...] = _dot(p.astype(BF16), vc_ref[...])

    def chunk(c, carry):
        off = pl.multiple_of(c * tk, tk)
        s = _dot_nt(q, k_ref[pl.ds(off, tk), :])
        m_prev = m_scr[...]
        m_new = jnp.maximum(m_prev, s.max(axis=-1, keepdims=True))
        alpha = jnp.exp(m_prev - m_new)
        p = jnp.exp(s - m_new)
        l_scr[...] = alpha * l_scr[...] + p.sum(axis=-1, keepdims=True)
        acc_scr[...] = alpha * acc_scr[...] + _dot(p.astype(BF16), v_ref[pl.ds(off, tk), :])
        m_scr[...] = m_new
        return carry

    lax.fori_loop(0, k_ref.shape[0] // tk, chunk, 0)
    _unstack_heads(o_ref, 0, acc_scr[...] / l_scr[...], tq)


def _attn_a(qa, kv, kv_ctx, batch, seq, ctx_len, tq=256, tk=512):
    nq = seq // tq
    gw = GROUP * HEAD_DIM
    rows = GROUP * tq
    return pl.pallas_call(
        functools.partial(_attn_a_kernel, tq, tk),
        out_shape=jax.ShapeDtypeStruct(qa.shape, BF16),
        grid=(batch, A_KV_HEADS, nq),
        in_specs=[
            pl.BlockSpec((tq, gw), lambda b, h, i: (b * nq + i, h)),
            pl.BlockSpec((ctx_len, HEAD_DIM), lambda b, h, i: (b, h)),
            pl.BlockSpec((ctx_len, HEAD_DIM), lambda b, h, i: (b, A_KV_HEADS + h)),
            pl.BlockSpec((seq, HEAD_DIM), lambda b, h, i: (b, h)),
            pl.BlockSpec((seq, HEAD_DIM), lambda b, h, i: (b, A_KV_HEADS + h)),
        ],
        out_specs=pl.BlockSpec((tq, gw), lambda b, h, i: (b * nq + i, h)),
        scratch_shapes=[
            pltpu.VMEM((rows, 1), F32),
            pltpu.VMEM((rows, 1), F32),
            pltpu.VMEM((rows, HEAD_DIM), F32),
        ],
        compiler_params=_params(("parallel", "parallel", "arbitrary")),
        name="attn_a",
    )(qa, kv_ctx, kv_ctx, kv, kv)


def _attn_b_kernel(seq, sink_ref, q_ref, kvc_ref, kv_ref, o_ref):
    i = pl.program_id(1)
    span = Q_BLOCK + 2 * WINDOW
    start = pl.multiple_of(jnp.clip(i * Q_BLOCK - WINDOW, 0, seq - span), Q_BLOCK)
    rows = GROUP * Q_BLOCK
    qpos = i * Q_BLOCK + lax.broadcasted_iota(jnp.int32, (rows, span), 0) % Q_BLOCK
    kpos = start + lax.broadcasted_iota(jnp.int32, (rows, span), 1)
    valid = jnp.abs(qpos - kpos) <= WINDOW
    for kvh in range(B_KV_HEADS):
        kc = slice(kvh * HEAD_DIM, (kvh + 1) * HEAD_DIM)
        vc = slice((B_KV_HEADS + kvh) * HEAD_DIM, (B_KV_HEADS + kvh + 1) * HEAD_DIM)
        q = _stack_heads(q_ref, kvh)
        s_win = jnp.where(valid, _dot_nt(q, kv_ref[pl.ds(start, span), kc]), NEG_INF)
        s_ctx = _dot_nt(q, kvc_ref[:, kc])
        o = _softmax_pv([s_win, s_ctx], [kv_ref[pl.ds(start, span), vc], kvc_ref[:, vc]],
                        _sink_column(sink_ref, kvh, Q_BLOCK))
        _unstack_heads(o_ref, kvh, o, Q_BLOCK)


def _attn_b(qb, kv, kv_ctx, sink, batch, seq, ctx_len):
    nq = seq // Q_BLOCK
    bw = 2 * B_KV_HEADS * HEAD_DIM
    return pl.pallas_call(
        functools.partial(_attn_b_kernel, seq),
        out_shape=jax.ShapeDtypeStruct(qb.shape, BF16),
        grid_spec=pltpu.PrefetchScalarGridSpec(
            num_scalar_prefetch=1,
            grid=(batch, nq),
            in_specs=[
                pl.BlockSpec((Q_BLOCK, Q_COLS), lambda b, i, s: (b * nq + i, 0)),
                pl.BlockSpec((ctx_len, bw), lambda b, i, s: (b, 1)),
                pl.BlockSpec((seq, bw), lambda b, i, s: (b, 1)),
            ],
            out_specs=pl.BlockSpec((Q_BLOCK, Q_COLS), lambda b, i, s: (b * nq + i, 0)),
        ),
        compiler_params=_params(("parallel", "arbitrary")),
        name="attn_b",
    )(sink, qb, kv_ctx, kv)


def _attn_ctx_kernel(ctx_len, sink_ref, qa_ref, qb_ref, kv_ref, oa_ref, ob_ref):
    for kvh in range(A_KV_HEADS):
        k = kv_ref[:, kvh * HEAD_DIM:(kvh + 1) * HEAD_DIM]
        v = kv_ref[:, (A_KV_HEADS + kvh) * HEAD_DIM:(A_KV_HEADS + kvh + 1) * HEAD_DIM]
        o = _softmax_pv([_dot_nt(_stack_heads(qa_ref, kvh), k)], [v])
        _unstack_heads(oa_ref, kvh, o, ctx_len)
    off = 2 * A_KV_HEADS
    for kvh in range(B_KV_HEADS):
        k = kv_ref[:, (off + kvh) * HEAD_DIM:(off + kvh + 1) * HEAD_DIM]
        v = kv_ref[:, (off + B_KV_HEADS + kvh) * HEAD_DIM:(off + B_KV_HEADS + kvh + 1) * HEAD_DIM]
        o = _softmax_pv([_dot_nt(_stack_heads(qb_ref, kvh), k)], [v], _sink_column(sink_ref, kvh, ctx_len))
        _unstack_heads(ob_ref, kvh, o, ctx_len)


def _attn_ctx(qa, qb, kv_ctx, sink, batch, ctx_len):
    q_spec = pl.BlockSpec((ctx_len, Q_COLS), lambda b, s: (b, 0))
    return pl.pallas_call(
        functools.partial(_attn_ctx_kernel, ctx_len),
        out_shape=[jax.ShapeDtypeStruct(qa.shape, BF16), jax.ShapeDtypeStruct(qb.shape, BF16)],
        grid_spec=pltpu.PrefetchScalarGridSpec(
            num_scalar_prefetch=1,
            grid=(batch,),
            in_specs=[q_spec, q_spec, pl.BlockSpec((ctx_len, KV_COLS), lambda b, s: (b, 0))],
            out_specs=[q_spec, q_spec],
        ),
        compiler_params=_params(("parallel",)),
        name="attn_ctx",
    )(sink, qa, qb, kv_ctx)


def _outproj_kernel(tm, blocks_per_seq, oa_ref, ob_ref, z_ref, zprev_ref, znext_ref, gp_ref, cw_ref, w_ref,
                    x_ref, gt_ref, gpost_ref, gffn_ref, sc_ref, sh_ref, xo_ref, h2_ref):
    i = pl.program_id(0)
    pos = i % blocks_per_seq
    z = z_ref[...].astype(F32)
    prev_row = jnp.where(pos == 0, 0.0, zprev_ref[BF16_SUBLANES - 1:BF16_SUBLANES, :].astype(F32))
    next_row = jnp.where(pos == blocks_per_seq - 1, 0.0, znext_ref[0:1, :].astype(F32))
    row = lax.broadcasted_iota(jnp.int32, z.shape, 0)
    z_prev = jnp.where(row == 0, prev_row, pltpu.roll(z, 1, 0))
    z_next = jnp.where(row == tm - 1, next_row, pltpu.roll(z, tm - 1, 0))
    cw = cw_ref[...]
    oc = gp_ref[...].astype(F32) * (z_prev * cw[0:1] + z * cw[1:2] + z_next * cw[2:3])

    y = _dot(oa_ref[...], w_ref[0:Q_COLS, :])
    y += _dot(ob_ref[...], w_ref[Q_COLS:2 * Q_COLS, :])
    y += _dot(oc.astype(BF16), w_ref[2 * Q_COLS:, :])
    x = x_ref[...] + gt_ref[...] * (_rms(y) * gpost_ref[...])
    xo_ref[...] = x
    h2 = _rms(x) * gffn_ref[...]
    h2_ref[...] = (h2 * (1.0 + sc_ref[...]) + sh_ref[...]).astype(BF16)


def _outproj(oa, ob, z, gp, conv_w, w_out, x, mod, row_of_block, g_post, g_ffn, tm, seq_len):
    m = x.shape[0]
    blocks_per_seq = seq_len // tm
    halo = tm // BF16_SUBLANES
    last_halo = m // BF16_SUBLANES - 1
    row_spec = lambda w: pl.BlockSpec((tm, w), lambda i: (i, 0))
    return pl.pallas_call(
        functools.partial(_outproj_kernel, tm, blocks_per_seq),
        out_shape=[jax.ShapeDtypeStruct((m, D_MODEL), F32), jax.ShapeDtypeStruct((m, D_MODEL), BF16)],
        grid=(m // tm,),
        in_specs=[
            row_spec(Q_COLS),
            row_spec(Q_COLS),
            row_spec(CONV_CH),
            pl.BlockSpec((BF16_SUBLANES, CONV_CH), lambda i: (jnp.maximum(i * halo - 1, 0), 0)),
            pl.BlockSpec((BF16_SUBLANES, CONV_CH), lambda i: (jnp.minimum((i + 1) * halo, last_halo), 0)),
            row_spec(CONV_CH),
            pl.BlockSpec(conv_w.shape, lambda i: (0, 0)),
            _resident((D_MODEL, D_MODEL), lambda i: (0, 0)),
            row_spec(D_MODEL),
            _mod_spec(2, row_of_block),
            _vec_spec(D_MODEL),
            _vec_spec(D_MODEL),
            _mod_spec(4, row_of_block),
            _mod_spec(3, row_of_block),
        ],
        out_specs=[row_spec(D_MODEL), row_spec(D_MODEL)],
        compiler_params=_params(("parallel",)),
        name="outproj",
    )(oa, ob, z, z, z, gp, conv_w, w_out, x, mod, g_post, g_ffn, mod, mod)


def _ffn_kernel(h_ref, wg_ref, wu_ref, wd_ref, x_ref, gt_ref, g_ref, o_ref, acc_ref):
    f = pl.program_id(1)
    h = h_ref[...]
    a = (_silu(_dot(h, wg_ref[...])) * _dot(h, wu_ref[...])).astype(BF16)
    part = _dot(a, wd_ref[...])

    @pl.when(f == 0)
    def _():
        acc_ref[...] = part

    @pl.when(f > 0)
    def _():
        acc_ref[...] += part

    @pl.when(f == pl.num_programs(1) - 1)
    def _():
        o_ref[...] = x_ref[...] + gt_ref[...] * (_rms(acc_ref[...]) * g_ref[...])


def _ffn(h2, w_gate, w_up, w_down, x, mod, row_of_block, g_post, tm, tf=512):
    m = x.shape[0]
    hidden = w_gate.shape[1]
    mod_spec = pl.BlockSpec((None, 1, D_MODEL), lambda i, f: (row_of_block(i) * N_MOD + 5, 0, 0))
    return pl.pallas_call(
        _ffn_kernel,
        out_shape=jax.ShapeDtypeStruct((m, D_MODEL), F32),
        grid=(m // tm, hidden // tf),
        in_specs=[
            pl.BlockSpec((tm, D_MODEL), lambda i, f: (i, 0)),
            pl.BlockSpec((D_MODEL, tf), lambda i, f: (0, f)),
            pl.BlockSpec((D_MODEL, tf), lambda i, f: (0, f)),
            pl.BlockSpec((tf, D_MODEL), lambda i, f: (f, 0)),
            pl.BlockSpec((tm, D_MODEL), lambda i, f: (i, 0)),
            mod_spec,
            pl.BlockSpec((1, D_MODEL), lambda i, f: (0, 0)),
        ],
        out_specs=pl.BlockSpec((tm, D_MODEL), lambda i, f: (i, 0)),
        scratch_shapes=[pltpu.VMEM((tm, D_MODEL), F32)],
        compiler_params=_params(("parallel", "arbitrary")),
        name="ffn",
    )(h2, w_gate, w_up, w_down, x, mod, g_post)


def _rope_tables(seq):
    pos = jnp.arange(seq, dtype=jnp.int32)
    inv_freq = jnp.power(ROPE_BASE, -jnp.arange(ROPE_PAIRS, dtype=F32) / ROPE_PAIRS)
    ang = jnp.stack([pos // GRID_W, pos % GRID_W], axis=1).astype(F32)[:, :, None] * inv_freq
    cos, sin = jnp.cos(ang), jnp.sin(ang)
    cos_t = jnp.concatenate([cos[:, 0], cos[:, 0], cos[:, 1], cos[:, 1]], axis=-1)
    sin_t = jnp.concatenate([-sin[:, 0], sin[:, 0], -sin[:, 1], sin[:, 1]], axis=-1)
    return cos_t, sin_t


def kernel(x, c, ctx, c_ctx, w_ada, b_ada, g_mix_pre, g_mix_post, g_ffn_pre, g_ffn_post, w_in, q_norm, k_norm,
           sink, conv_w, w_out, w_gate, w_up, w_down):
    batch, seq, d = x.shape
    ctx_len = ctx.shape[1]
    depth = w_ada.shape[0]
    assert d == D_MODEL and batch < MOD_ROWS and seq % 512 == 0 and ctx_len % Q_BLOCK == 0

    tm = 512
    tm_ctx = ctx_len
    lat_row = lambda i: i // (seq // tm)
    ctx_row = lambda i: batch

    cond = jnp.zeros((MOD_ROWS, d), F32).at[:batch].set(c).at[batch].set(c_ctx)
    mod_all = _ada(cond, w_ada, b_ada).reshape(depth, MOD_ROWS * N_MOD, 1, d)
    tables = _rope_tables(seq)

    xl = x.reshape(batch * seq, d)
    xc = ctx.reshape(batch * ctx_len, d)
    vec = lambda a, l: a[l].reshape(1, -1)
    for l in range(depth):
        mod = mod_all[l]
        w_in_l = w_in[l].astype(BF16)
        w_out_l = w_out[l].astype(BF16)
        wg, wu, wd = w_gate[l].astype(BF16), w_up[l].astype(BF16), w_down[l].astype(BF16)
        g_pre, g_post = vec(g_mix_pre, l), vec(g_mix_post, l)
        g_ffn, g_ffn_out = vec(g_ffn_pre, l), vec(g_ffn_post, l)
        qn, kn = vec(q_norm, l), vec(k_norm, l)
        update_ctx = l < depth - 1

        kv_c, qa_c, qb_c, z_c, gp_c = _inproj(xc, mod, ctx_row, g_pre, w_in_l, qn, kn, None, tm_ctx, ctx_len)
        kv, qa, qb, z, gp = _inproj(xl, mod, lat_row, g_pre, w_in_l, qn, kn, tables, tm, seq)
        oa = _attn_a(qa, kv, kv_c, batch, seq, ctx_len)
        ob = _attn_b(qb, kv, kv_c, sink[l], batch, seq, ctx_len)
        xl, h2 = _outproj(oa, ob, z, gp, conv_w[l], w_out_l, xl, mod, lat_row, g_post, g_ffn, tm, seq)
        xl = _ffn(h2, wg, wu, wd, xl, mod, lat_row, g_ffn_out, tm)
        if update_ctx:
            oa_c, ob_c = _attn_ctx(qa_c, qb_c, kv_c, sink[l], batch, ctx_len)
            xc, h2_c = _outproj(oa_c, ob_c, z_c, gp_c, conv_w[l], w_out_l, xc, mod, ctx_row, g_post, g_ffn,
                                tm_ctx, ctx_len)
            xc = _ffn(h2_c, wg, wu, wd, xc, mod, ctx_row, g_ffn_out, tm_ctx)
    return xl.reshape(batch, seq, d)
```

```python
import functools

import jax
import jax.numpy as jnp
from jax import lax
from jax.experimental import pallas as pl
from jax.experimental.pallas import tpu as pltpu

F32 = jnp.float32
BF16 = jnp.bfloat16

D_MODEL = 2048
HEAD_DIM = 128
GRID_W = 64
A_Q_HEADS = 6
A_KV_HEADS = 2
B_Q_HEADS = 6
B_KV_HEADS = 2
GROUP = A_Q_HEADS // A_KV_HEADS
Q_COLS = A_Q_HEADS * HEAD_DIM
CONV_CH = D_MODEL - 2 * Q_COLS
WINDOW = 128
Q_BLOCK = 128
ROPE_BASE = 10000.0
ROPE_PAIRS = HEAD_DIM // 4
NORM_EPS = 1e-6
NEG_INF = -1e30
ATTN_SCALE = HEAD_DIM ** -0.5
LOG2_E = 1.4426950408889634
SCORE_SCALE = ATTN_SCALE * LOG2_E
N_MOD = 6
MOD_ROWS = 8

KV_COLS = 4 * A_KV_HEADS * HEAD_DIM
OFF_AQ = KV_COLS
OFF_BQ = OFF_AQ + Q_COLS
OFF_CONV = OFF_BQ + Q_COLS
IN_COLS = OFF_CONV + 3 * CONV_CH

V7X_VMEM_BYTES = 64 * 1024 * 1024
VMEM_LIMIT = 56 * 1024 * 1024
BF16_SUBLANES = 16


def _rms(t):
    return t * lax.rsqrt(jnp.mean(t * t, axis=-1, keepdims=True) + NORM_EPS)


def _silu(t):
    return t * (1.0 / (1.0 + jnp.exp(-t)))


def _dot(a, b):
    return jnp.dot(a, b, preferred_element_type=F32)


def _dot_nt(a, b):
    return lax.dot_general(a, b, (((1,), (1,)), ((), ())), preferred_element_type=F32)


def _params(semantics):
    return pltpu.CompilerParams(dimension_semantics=semantics, vmem_limit_bytes=VMEM_LIMIT)


def _resident(shape, index_map):
    return pl.BlockSpec(shape, index_map, pipeline_mode=pl.Buffered(1))


def _ada_kernel(s_ref, w_ref, b_ref, o_ref):
    a = _silu(s_ref[...]).astype(BF16)
    o_ref[0] = _dot(a, w_ref[0].astype(BF16)) + b_ref[0]


def _ada(cond, w_ada, b_ada, tn=1024):
    depth, d, n = w_ada.shape
    return pl.pallas_call(
        _ada_kernel,
        out_shape=jax.ShapeDtypeStruct((depth, MOD_ROWS, n), F32),
        grid=(depth, n // tn),
        in_specs=[
            pl.BlockSpec((MOD_ROWS, d), lambda l, j: (0, 0)),
            pl.BlockSpec((1, d, tn), lambda l, j: (l, 0, j)),
            pl.BlockSpec((1, 1, tn), lambda l, j: (l, 0, j)),
        ],
        out_specs=pl.BlockSpec((1, MOD_ROWS, tn), lambda l, j: (l, 0, j)),
        compiler_params=_params(("parallel", "parallel")),
        name="ada",
    )(cond, w_ada, b_ada.reshape(depth, 1, n))


def _mod_spec(chunk, row_of_block):
    return pl.BlockSpec((None, 1, D_MODEL), lambda i: (row_of_block(i) * N_MOD + chunk, 0, 0))


def _vec_spec(n):
    return pl.BlockSpec((1, n), lambda i: (0, 0))


def _inproj_kernel(rope, x_ref, g_ref, sc_ref, sh_ref, w_ref, qn_ref, kn_ref, *rest):
    if rope:
        cos_ref, sin_ref, kv_ref, qa_ref, qb_ref, z_ref, gp_ref = rest
        cos = cos_ref[...]
        sin = sin_ref[...]
        lane = lax.broadcasted_iota(jnp.int32, cos.shape, 1)
        first_half = (lane & ROPE_PAIRS) == 0

        def rot(t):
            partner = jnp.where(first_half, pltpu.roll(t, HEAD_DIM - ROPE_PAIRS, 1), pltpu.roll(t, ROPE_PAIRS, 1))
            return t * cos + partner * sin
    else:
        kv_ref, qa_ref, qb_ref, z_ref, gp_ref = rest

        def rot(t):
            return t

    h = _rms(x_ref[...]) * g_ref[...]
    hb = (h * (1.0 + sc_ref[...]) + sh_ref[...]).astype(BF16)

    def proj(c0, n):
        return _dot(hb, w_ref[:, c0:c0 + n])

    def head(t, k):
        return t[:, k * HEAD_DIM:(k + 1) * HEAD_DIM]

    def put(ref, k, t):
        ref[:, k * HEAD_DIM:(k + 1) * HEAD_DIM] = t.astype(BF16)

    pkv = proj(0, KV_COLS)
    k_gain = kn_ref[...] * SCORE_SCALE
    for k in range(A_KV_HEADS):
        put(kv_ref, k, rot(_rms(head(pkv, k)) * k_gain))
        put(kv_ref, A_KV_HEADS + k, head(pkv, A_KV_HEADS + k))
        put(kv_ref, 2 * A_KV_HEADS + k, rot(head(pkv, 2 * A_KV_HEADS + k)) * SCORE_SCALE)
        put(kv_ref, 3 * A_KV_HEADS + k, head(pkv, 3 * A_KV_HEADS + k))

    pq = proj(OFF_AQ, Q_COLS)
    q_gain = qn_ref[...]
    for k in range(A_Q_HEADS):
        put(qa_ref, k, rot(_rms(head(pq, k)) * q_gain))
    pq = proj(OFF_BQ, Q_COLS)
    for k in range(B_Q_HEADS):
        put(qb_ref, k, rot(head(pq, k)))

    z_ref[...] = (proj(OFF_CONV, CONV_CH) * proj(OFF_CONV + 2 * CONV_CH, CONV_CH)).astype(BF16)
    gp_ref[...] = proj(OFF_CONV + CONV_CH, CONV_CH).astype(BF16)


def _inproj(x, mod, row_of_block, g_pre, w_in, q_norm, k_norm, rope_tables, tm, seq_len):
    m = x.shape[0]
    rope = rope_tables is not None
    in_specs = [
        pl.BlockSpec((tm, D_MODEL), lambda i: (i, 0)),
        _vec_spec(D_MODEL),
        _mod_spec(1, row_of_block),
        _mod_spec(0, row_of_block),
        _resident((D_MODEL, IN_COLS), lambda i: (0, 0)),
        _vec_spec(HEAD_DIM),
        _vec_spec(HEAD_DIM),
    ]
    args = [x, g_pre, mod, mod, w_in, q_norm, k_norm]
    if rope:
        blocks_per_seq = seq_len // tm
        in_specs += [pl.BlockSpec((tm, HEAD_DIM), lambda i: (i % blocks_per_seq, 0))] * 2
        args += list(rope_tables)
    widths = (KV_COLS, Q_COLS, Q_COLS, CONV_CH, CONV_CH)
    return pl.pallas_call(
        functools.partial(_inproj_kernel, rope),
        out_shape=[jax.ShapeDtypeStruct((m, w), BF16) for w in widths],
        grid=(m // tm,),
        in_specs=in_specs,
        out_specs=[pl.BlockSpec((tm, w), lambda i: (i, 0)) for w in widths],
        compiler_params=_params(("parallel",)),
        name="inproj_rope" if rope else "inproj",
    )(*args)


def _stack_heads(q_ref, kvh):
    base = kvh * GROUP * HEAD_DIM
    return jnp.concatenate(
        [q_ref[:, base + g * HEAD_DIM: base + (g + 1) * HEAD_DIM] for g in range(GROUP)], axis=0)


def _unstack_heads(o_ref, kvh, o, tq):
    base = kvh * GROUP * HEAD_DIM
    for g in range(GROUP):
        o_ref[:, base + g * HEAD_DIM: base + (g + 1) * HEAD_DIM] = o[g * tq:(g + 1) * tq].astype(BF16)


def _sink_column(sink_ref, kvh, tq):
    grp = lax.broadcasted_iota(jnp.int32, (GROUP * tq, 1), 0) // tq
    col = jnp.full((GROUP * tq, 1), sink_ref[kvh * GROUP], F32)
    for g in range(1, GROUP):
        col = jnp.where(grp == g, sink_ref[kvh * GROUP + g], col)
    return col * LOG2_E


def _softmax_pv(scores, values, sink_col=None):
    m = scores[0].max(axis=-1, keepdims=True)
    for s in scores[1:]:
        m = jnp.maximum(m, s.max(axis=-1, keepdims=True))
    if sink_col is not None:
        m = jnp.maximum(m, sink_col)
    den = None
    acc = None
    for s, v in zip(scores, values):
        p = jnp.exp2(s - m)
        d = p.sum(axis=-1, keepdims=True)
        a = _dot(p.astype(BF16), v)
        den = d if den is None else den + d
        acc = a if acc is None else acc + a
    if sink_col is not None:
        den = den + jnp.exp2(sink_col - m)
    return acc / den


def _dot_tn(a, b):
    return lax.dot_general(a, b, (((0,), (0,)), ((), ())), preferred_element_type=F32)


def _attn_a_kernel(tq, tk, q_ref, kc_ref, vc_ref, k_ref, v_ref, o_ref, acc_scr):
    q = _stack_heads(q_ref, 0)
    m = l = None
    for c in range(-1, k_ref.shape[0] // tk):
        if c < 0:
            k, v = kc_ref[...], vc_ref[...]
        else:
            k, v = k_ref[c * tk:(c + 1) * tk, :], v_ref[c * tk:(c + 1) * tk, :]
        s = _dot_nt(k, q)
        m_new = s.max(axis=0, keepdims=True)
        if m is not None:
            m_new = jnp.maximum(m, m_new)
        p = jnp.exp2(s - m_new)
        l_new = p.sum(axis=0, keepdims=True)
        pv = _dot_tn(v, p.astype(BF16))
        if m is None:
            acc_scr[...] = pv
        else:
            alpha = jnp.exp2(m - m_new)
            l_new = alpha * l + l_new
            acc_scr[...] = alpha * acc_scr[...] + pv
        m, l = m_new, l_new
    o = acc_scr[...] / l
    for g in range(GROUP):
        o_ref[:, g * HEAD_DIM:(g + 1) * HEAD_DIM] = o[:, g * tq:(g + 1) * tq].T.astype(BF16)


def _attn_a(qa, kv, kv_ctx, batch, seq, ctx_len, tq=256, tk=512):
    nq = seq // tq
    gw = GROUP * HEAD_DIM
    return pl.pallas_call(
        functools.partial(_attn_a_kernel, tq, tk),
        out_shape=jax.ShapeDtypeStruct(qa.shape, BF16),
        grid=(batch, A_KV_HEADS, nq),
        in_specs=[
            pl.BlockSpec((tq, gw), lambda b, h, i: (b * nq + i, h)),
            pl.BlockSpec((ctx_len, HEAD_DIM), lambda b, h, i: (b, h)),
            pl.BlockSpec((ctx_len, HEAD_DIM), lambda b, h, i: (b, A_KV_HEADS + h)),
            pl.BlockSpec((seq, HEAD_DIM), lambda b, h, i: (b, h)),
            pl.BlockSpec((seq, HEAD_DIM), lambda b, h, i: (b, A_KV_HEADS + h)),
        ],
        out_specs=pl.BlockSpec((tq, gw), lambda b, h, i: (b * nq + i, h)),
        scratch_shapes=[pltpu.VMEM((HEAD_DIM, GROUP * tq), F32)],
        compiler_params=_params(("parallel", "parallel", "arbitrary")),
        name="attn_a",
    )(qa, kv_ctx, kv_ctx, kv, kv)


def _attn_b_kernel(seq, sink_ref, q_ref, kvc_ref, kv_ref, o_ref):
    i = pl.program_id(1)
    span = Q_BLOCK + 2 * WINDOW
    start = pl.multiple_of(jnp.clip(i * Q_BLOCK - WINDOW, 0, seq - span), Q_BLOCK)
    rows = GROUP * Q_BLOCK
    qpos = i * Q_BLOCK + lax.broadcasted_iota(jnp.int32, (rows, span), 0) % Q_BLOCK
    kpos = start + lax.broadcasted_iota(jnp.int32, (rows, span), 1)
    valid = jnp.abs(qpos - kpos) <= WINDOW
    for kvh in range(B_KV_HEADS):
        kc = slice(kvh * HEAD_DIM, (kvh + 1) * HEAD_DIM)
        vc = slice((B_KV_HEADS + kvh) * HEAD_DIM, (B_KV_HEADS + kvh + 1) * HEAD_DIM)
        q = _stack_heads(q_ref, kvh)
        s_win = jnp.where(valid, _dot_nt(q, kv_ref[pl.ds(start, span), kc]), NEG_INF)
        s_ctx = _dot_nt(q, kvc_ref[:, kc])
        o = _softmax_pv([s_win, s_ctx], [kv_ref[pl.ds(start, span), vc], kvc_ref[:, vc]],
                        _sink_column(sink_ref, kvh, Q_BLOCK))
        _unstack_heads(o_ref, kvh, o, Q_BLOCK)


def _attn_b(qb, kv, kv_ctx, sink, batch, seq, ctx_len):
    nq = seq // Q_BLOCK
    bw = 2 * B_KV_HEADS * HEAD_DIM
    return pl.pallas_call(
        functools.partial(_attn_b_kernel, seq),
        out_shape=jax.ShapeDtypeStruct(qb.shape, BF16),
        grid_spec=pltpu.PrefetchScalarGridSpec(
            num_scalar_prefetch=1,
            grid=(batch, nq),
            in_specs=[
                pl.BlockSpec((Q_BLOCK, Q_COLS), lambda b, i, s: (b * nq + i, 0)),
                pl.BlockSpec((ctx_len, bw), lambda b, i, s: (b, 1)),
                pl.BlockSpec((seq, bw), lambda b, i, s: (b, 1)),
            ],
            out_specs=pl.BlockSpec((Q_BLOCK, Q_COLS), lambda b, i, s: (b * nq + i, 0)),
        ),
        compiler_params=_params(("parallel", "arbitrary")),
        name="attn_b",
    )(sink, qb, kv_ctx, kv)


def _attn_ctx_kernel(ctx_len, sink_ref, qa_ref, qb_ref, kv_ref, oa_ref, ob_ref):
    for kvh in range(A_KV_HEADS):
        k = kv_ref[:, kvh * HEAD_DIM:(kvh + 1) * HEAD_DIM]
        v = kv_ref[:, (A_KV_HEADS + kvh) * HEAD_DIM:(A_KV_HEADS + kvh + 1) * HEAD_DIM]
        o = _softmax_pv([_dot_nt(_stack_heads(qa_ref, kvh), k)], [v])
        _unstack_heads(oa_ref, kvh, o, ctx_len)
    off = 2 * A_KV_HEADS
    for kvh in range(B_KV_HEADS):
        k = kv_ref[:, (off + kvh) * HEAD_DIM:(off + kvh + 1) * HEAD_DIM]
        v = kv_ref[:, (off + B_KV_HEADS + kvh) * HEAD_DIM:(off + B_KV_HEADS + kvh + 1) * HEAD_DIM]
        o = _softmax_pv([_dot_nt(_stack_heads(qb_ref, kvh), k)], [v], _sink_column(sink_ref, kvh, ctx_len))
        _unstack_heads(ob_ref, kvh, o, ctx_len)


def _attn_ctx(qa, qb, kv_ctx, sink, batch, ctx_len):
    q_spec = pl.BlockSpec((ctx_len, Q_COLS), lambda b, s: (b, 0))
    return pl.pallas_call(
        functools.partial(_attn_ctx_kernel, ctx_len),
        out_shape=[jax.ShapeDtypeStruct(qa.shape, BF16), jax.ShapeDtypeStruct(qb.shape, BF16)],
        grid_spec=pltpu.PrefetchScalarGridSpec(
            num_scalar_prefetch=1,
            grid=(batch,),
            in_specs=[q_spec, q_spec, pl.BlockSpec((ctx_len, KV_COLS), lambda b, s: (b, 0))],
            out_specs=[q_spec, q_spec],
        ),
        compiler_params=_params(("parallel",)),
        name="attn_ctx",
    )(sink, qa, qb, kv_ctx)


def _outproj_kernel(tm, blocks_per_seq, oa_ref, ob_ref, z_ref, zprev_ref, znext_ref, gp_ref, cw_ref, w_ref,
                    x_ref, gt_ref, gpost_ref, gffn_ref, sc_ref, sh_ref, xo_ref, h2_ref):
    i = pl.program_id(0)
    pos = i % blocks_per_seq
    z = z_ref[...].astype(F32)
    prev_row = jnp.where(pos == 0, 0.0, zprev_ref[BF16_SUBLANES - 1:BF16_SUBLANES, :].astype(F32))
    next_row = jnp.where(pos == blocks_per_seq - 1, 0.0, znext_ref[0:1, :].astype(F32))
    row = lax.broadcasted_iota(jnp.int32, z.shape, 0)
    z_prev = jnp.where(row == 0, prev_row, pltpu.roll(z, 1, 0))
    z_next = jnp.where(row == tm - 1, next_row, pltpu.roll(z, tm - 1, 0))
    cw = cw_ref[...]
    oc = gp_ref[...].astype(F32) * (z_prev * cw[0:1] + z * cw[1:2] + z_next * cw[2:3])

    y = _dot(oa_ref[...], w_ref[0:Q_COLS, :])
    y += _dot(ob_ref[...], w_ref[Q_COLS:2 * Q_COLS, :])
    y += _dot(oc.astype(BF16), w_ref[2 * Q_COLS:, :])
    x = x_ref[...] + gt_ref[...] * (_rms(y) * gpost_ref[...])
    xo_ref[...] = x
    h2 = _rms(x) * gffn_ref[...]
    h2_ref[...] = (h2 * (1.0 + sc_ref[...]) + sh_ref[...]).astype(BF16)


def _outproj(oa, ob, z, gp, conv_w, w_out, x, mod, row_of_block, g_post, g_ffn, tm, seq_len):
    m = x.shape[0]
    blocks_per_seq = seq_len // tm
    halo = tm // BF16_SUBLANES
    last_halo = m // BF16_SUBLANES - 1
    row_spec = lambda w: pl.BlockSpec((tm, w), lambda i: (i, 0))
    return pl.pallas_call(
        functools.partial(_outproj_kernel, tm, blocks_per_seq),
        out_shape=[jax.ShapeDtypeStruct((m, D_MODEL), F32), jax.ShapeDtypeStruct((m, D_MODEL), BF16)],
        grid=(m // tm,),
        in_specs=[
            row_spec(Q_COLS),
            row_spec(Q_COLS),
            row_spec(CONV_CH),
            pl.BlockSpec((BF16_SUBLANES, CONV_CH), lambda i: (jnp.maximum(i * halo - 1, 0), 0)),
            pl.BlockSpec((BF16_SUBLANES, CONV_CH), lambda i: (jnp.minimum((i + 1) * halo, last_halo), 0)),
            row_spec(CONV_CH),
            pl.BlockSpec(conv_w.shape, lambda i: (0, 0)),
            _resident((D_MODEL, D_MODEL), lambda i: (0, 0)),
            row_spec(D_MODEL),
            _mod_spec(2, row_of_block),
            _vec_spec(D_MODEL),
            _vec_spec(D_MODEL),
            _mod_spec(4, row_of_block),
            _mod_spec(3, row_of_block),
        ],
        out_specs=[row_spec(D_MODEL), row_spec(D_MODEL)],
        compiler_params=_params(("parallel",)),
        name="outproj",
    )(oa, ob, z, z, z, gp, conv_w, w_out, x, mod, g_post, g_ffn, mod, mod)


def _ffn_kernel(h_ref, wg_ref, wu_ref, wd_ref, x_ref, gt_ref, g_ref, o_ref, acc_ref):
    f = pl.program_id(1)

    @pl.when(f == 0)
    def _():
        acc_ref[...] = jnp.zeros_like(acc_ref)

    h = h_ref[...]
    a = (_silu(_dot(h, wg_ref[...])) * _dot(h, wu_ref[...])).astype(BF16)
    acc_ref[...] += _dot(a, wd_ref[...])

    @pl.when(f == pl.num_programs(1) - 1)
    def _():
        o_ref[...] = x_ref[...] + gt_ref[...] * (_rms(acc_ref[...]) * g_ref[...])


def _ffn(h2, w_gate, w_up, w_down, x, mod, row_of_block, g_post, tm, tf=512):
    m = x.shape[0]
    hidden = w_gate.shape[1]
    mod_spec = pl.BlockSpec((None, 1, D_MODEL), lambda i, f: (row_of_block(i) * N_MOD + 5, 0, 0))
    return pl.pallas_call(
        _ffn_kernel,
        out_shape=jax.ShapeDtypeStruct((m, D_MODEL), F32),
        grid=(m // tm, hidden // tf),
        in_specs=[
            pl.BlockSpec((tm, D_MODEL), lambda i, f: (i, 0)),
            pl.BlockSpec((D_MODEL, tf), lambda i, f: (0, f)),
            pl.BlockSpec((D_MODEL, tf), lambda i, f: (0, f)),
            pl.BlockSpec((tf, D_MODEL), lambda i, f: (f, 0)),
            pl.BlockSpec((tm, D_MODEL), lambda i, f: (i, 0)),
            mod_spec,
            pl.BlockSpec((1, D_MODEL), lambda i, f: (0, 0)),
        ],
        out_specs=pl.BlockSpec((tm, D_MODEL), lambda i, f: (i, 0)),
        scratch_shapes=[pltpu.VMEM((tm, D_MODEL), F32)],
        compiler_params=_params(("parallel", "arbitrary")),
        name="ffn",
    )(h2, w_gate, w_up, w_down, x, mod, g_post)


def _rope_tables(seq):
    pos = jnp.arange(seq, dtype=jnp.int32)
    inv_freq = jnp.power(ROPE_BASE, -jnp.arange(ROPE_PAIRS, dtype=F32) / ROPE_PAIRS)
    ang = jnp.stack([pos // GRID_W, pos % GRID_W], axis=1).astype(F32)[:, :, None] * inv_freq
    cos, sin = jnp.cos(ang), jnp.sin(ang)
    cos_t = jnp.concatenate([cos[:, 0], cos[:, 0], cos[:, 1], cos[:, 1]], axis=-1)
    sin_t = jnp.concatenate([-sin[:, 0], sin[:, 0], -sin[:, 1], sin[:, 1]], axis=-1)
    return cos_t, sin_t


def kernel(x, c, ctx, c_ctx, w_ada, b_ada, g_mix_pre, g_mix_post, g_ffn_pre, g_ffn_post, w_in, q_norm, k_norm,
           sink, conv_w, w_out, w_gate, w_up, w_down):
    batch, seq, d = x.shape
    ctx_len = ctx.shape[1]
    depth = w_ada.shape[0]
    assert d == D_MODEL and batch < MOD_ROWS and seq % 512 == 0 and ctx_len % Q_BLOCK == 0

    tm = 512
    tm_ctx = ctx_len
    lat_row = lambda i: i // (seq // tm)
    ctx_row = lambda i: batch

    cond = jnp.zeros((MOD_ROWS, d), F32).at[:batch].set(c).at[batch].set(c_ctx)
    mod_all = _ada(cond, w_ada, b_ada).reshape(depth, MOD_ROWS * N_MOD, 1, d)
    tables = _rope_tables(seq)

    xl = x.reshape(batch * seq, d)
    xc = ctx.reshape(batch * ctx_len, d)
    vec = lambda a, l: a[l].reshape(1, -1)
    for l in range(depth):
        mod = mod_all[l]
        w_in_l = w_in[l].astype(BF16)
        w_out_l = w_out[l].astype(BF16)
        wg, wu, wd = w_gate[l].astype(BF16), w_up[l].astype(BF16), w_down[l].astype(BF16)
        g_pre, g_post = vec(g_mix_pre, l), vec(g_mix_post, l)
        g_ffn, g_ffn_out = vec(g_ffn_pre, l), vec(g_ffn_post, l)
        qn, kn = vec(q_norm, l), vec(k_norm, l)
        update_ctx = l < depth - 1

        kv_c, qa_c, qb_c, z_c, gp_c = _inproj(xc, mod, ctx_row, g_pre, w_in_l, qn, kn, None, tm_ctx, ctx_len)
        kv, qa, qb, z, gp = _inproj(xl, mod, lat_row, g_pre, w_in_l, qn, kn, tables, tm, seq)
        oa = _attn_a(qa, kv, kv_c, batch, seq, ctx_len)
        ob = _attn_b(qb, kv, kv_c, sink[l], batch, seq, ctx_len)
        xl, h2 = _outproj(oa, ob, z, gp, conv_w[l], w_out_l, xl, mod, lat_row, g_post, g_ffn, tm, seq)
        xl = _ffn(h2, wg, wu, wd, xl, mod, lat_row, g_ffn_out, tm)
        if update_ctx:
            oa_c, ob_c = _attn_ctx(qa_c, qb_c, kv_c, sink[l], batch, ctx_len)
            xc, h2_c = _outproj(oa_c, ob_c, z_c, gp_c, conv_w[l], w_out_l, xc, mod, ctx_row, g_post, g_ffn,
                                tm_ctx, ctx_len)
            xc = _ffn(h2_c, wg, wu, wd, xc, mod, ctx_row, g_ffn_out, tm_ctx)
    return xl.reshape(batch, seq, d)
```

```python
import functools

import jax
import jax.numpy as jnp
from jax import lax
from jax.experimental import pallas as pl
from jax.experimental.pallas import tpu as pltpu

F32 = jnp.float32
BF16 = jnp.bfloat16

D_MODEL = 2048
HEAD_DIM = 128
GRID_W = 64
A_Q_HEADS = 6
A_KV_HEADS = 2
B_Q_HEADS = 6
B_KV_HEADS = 2
GROUP = A_Q_HEADS // A_KV_HEADS
Q_COLS = A_Q_HEADS * HEAD_DIM
CONV_CH = D_MODEL - 2 * Q_COLS
WINDOW = 128
Q_BLOCK = 128
ROPE_BASE = 10000.0
ROPE_PAIRS = HEAD_DIM // 4
NORM_EPS = 1e-6
NEG_INF = -1e30
ATTN_SCALE = HEAD_DIM ** -0.5
LOG2_E = 1.4426950408889634
SCORE_SCALE = ATTN_SCALE * LOG2_E
N_MOD = 6
MOD_ROWS = 8

KV_COLS = 4 * A_KV_HEADS * HEAD_DIM
OFF_AQ = KV_COLS
OFF_BQ = OFF_AQ + Q_COLS
OFF_CONV = OFF_BQ + Q_COLS
IN_COLS = OFF_CONV + 3 * CONV_CH

VMEM_LIMIT = 56 * 1024 * 1024
BF16_SUBLANES = 16
ROW_SUB = 256


def _rms(t):
    return t * lax.rsqrt(jnp.mean(t * t, axis=-1, keepdims=True) + NORM_EPS)


def _silu(t):
    return t * (1.0 / (1.0 + jnp.exp(-t)))


def _dot(a, b):
    return jnp.dot(a, b, preferred_element_type=F32)


def _dot_nt(a, b):
    return lax.dot_general(a, b, (((1,), (1,)), ((), ())), preferred_element_type=F32)


def _params(semantics):
    return pltpu.CompilerParams(dimension_semantics=semantics, vmem_limit_bytes=VMEM_LIMIT)


def _resident(shape, index_map):
    return pl.BlockSpec(shape, index_map, pipeline_mode=pl.Buffered(1))


def _ada_kernel(s_ref, w_ref, b_ref, o_ref):
    a = _silu(s_ref[...]).astype(BF16)
    o_ref[0] = _dot(a, w_ref[0].astype(BF16)) + b_ref[0]


def _ada(cond, w_ada, b_ada, tn=1024):
    depth, d, n = w_ada.shape
    return pl.pallas_call(
        _ada_kernel,
        out_shape=jax.ShapeDtypeStruct((depth, MOD_ROWS, n), F32),
        grid=(depth, n // tn),
        in_specs=[
            pl.BlockSpec((MOD_ROWS, d), lambda l, j: (0, 0)),
            pl.BlockSpec((1, d, tn), lambda l, j: (l, 0, j)),
            pl.BlockSpec((1, 1, tn), lambda l, j: (l, 0, j)),
        ],
        out_specs=pl.BlockSpec((1, MOD_ROWS, tn), lambda l, j: (l, 0, j)),
        compiler_params=_params(("parallel", "parallel")),
        name="ada",
    )(cond, w_ada, b_ada.reshape(depth, 1, n))


def _mod_spec(chunk, row_of_block):
    return pl.BlockSpec((None, 1, D_MODEL), lambda i: (row_of_block(i) * N_MOD + chunk, 0, 0))


def _vec_spec(n):
    return pl.BlockSpec((1, n), lambda i: (0, 0))


def _inproj_kernel(rope, tm, x_ref, g_ref, sc_ref, sh_ref, w_ref, qn_ref, kn_ref, *rest):
    if rope:
        cos_ref, sin_ref, kv_ref, qa_ref, qb_ref, z_ref, gp_ref = rest
    else:
        kv_ref, qa_ref, qb_ref, z_ref, gp_ref = rest
    sub = min(tm, ROW_SUB)
    pre_gain = g_ref[...] * (1.0 + sc_ref[...])
    shift = sh_ref[...]
    k_gain = kn_ref[...] * SCORE_SCALE
    q_gain = qn_ref[...]
    lane = lax.broadcasted_iota(jnp.int32, (sub, HEAD_DIM), 1)
    first_half = (lane & ROPE_PAIRS) == 0

    for r in range(tm // sub):
        rows = slice(r * sub, (r + 1) * sub)
        if rope:
            cos = cos_ref[rows, :]
            sin = sin_ref[rows, :]

            def rot(t):
                partner = jnp.where(first_half, pltpu.roll(t, HEAD_DIM - ROPE_PAIRS, 1),
                                    pltpu.roll(t, ROPE_PAIRS, 1))
                return t * cos + partner * sin
        else:
            def rot(t):
                return t

        hb = (_rms(x_ref[rows, :]) * pre_gain + shift).astype(BF16)

        def proj(c0, n):
            return _dot(hb, w_ref[:, c0:c0 + n])

        def head(t, k):
            return t[:, k * HEAD_DIM:(k + 1) * HEAD_DIM]

        def put(ref, k, t):
            ref[rows, k * HEAD_DIM:(k + 1) * HEAD_DIM] = t.astype(BF16)

        pkv = proj(0, KV_COLS)
        for k in range(A_KV_HEADS):
            put(kv_ref, k, rot(_rms(head(pkv, k)) * k_gain))
            put(kv_ref, A_KV_HEADS + k, head(pkv, A_KV_HEADS + k))
            put(kv_ref, 2 * A_KV_HEADS + k, rot(head(pkv, 2 * A_KV_HEADS + k)) * SCORE_SCALE)
            put(kv_ref, 3 * A_KV_HEADS + k, head(pkv, 3 * A_KV_HEADS + k))

        pq = proj(OFF_AQ, Q_COLS)
        for k in range(A_Q_HEADS):
            put(qa_ref, k, rot(_rms(head(pq, k)) * q_gain))
        pq = proj(OFF_BQ, Q_COLS)
        for k in range(B_Q_HEADS):
            put(qb_ref, k, rot(head(pq, k)))

        z_ref[rows, :] = (proj(OFF_CONV, CONV_CH) * proj(OFF_CONV + 2 * CONV_CH, CONV_CH)).astype(BF16)
        gp_ref[rows, :] = proj(OFF_CONV + CONV_CH, CONV_CH).astype(BF16)


def _inproj(x, mod, row_of_block, g_pre, w_in, layer, q_norm, k_norm, rope_tables, tm, seq_len):
    m = x.shape[0]
    rope = rope_tables is not None
    in_specs = [
        pl.BlockSpec((tm, D_MODEL), lambda i: (i, 0)),
        _vec_spec(D_MODEL),
        _mod_spec(1, row_of_block),
        _mod_spec(0, row_of_block),
        _resident((None, D_MODEL, IN_COLS), lambda i: (layer, 0, 0)),
        _vec_spec(HEAD_DIM),
        _vec_spec(HEAD_DIM),
    ]
    args = [x, g_pre, mod, mod, w_in, q_norm, k_norm]
    if rope:
        blocks_per_seq = seq_len // tm
        in_specs += [pl.BlockSpec((tm, HEAD_DIM), lambda i: (i % blocks_per_seq, 0))] * 2
        args += list(rope_tables)
    widths = (KV_COLS, Q_COLS, Q_COLS, CONV_CH, CONV_CH)
    return pl.pallas_call(
        functools.partial(_inproj_kernel, rope, tm),
        out_shape=[jax.ShapeDtypeStruct((m, w), BF16) for w in widths],
        grid=(m // tm,),
        in_specs=in_specs,
        out_specs=[pl.BlockSpec((tm, w), lambda i: (i, 0)) for w in widths],
        compiler_params=_params(("parallel",)),
        name="inproj_rope" if rope else "inproj",
    )(*args)


def _stack_heads(q_ref, kvh):
    base = kvh * GROUP * HEAD_DIM
    return jnp.concatenate(
        [q_ref[:, base + g * HEAD_DIM: base + (g + 1) * HEAD_DIM] for g in range(GROUP)], axis=0)


def _unstack_heads(o_ref, kvh, o, tq):
    base = kvh * GROUP * HEAD_DIM
    for g in range(GROUP):
        o_ref[:, base + g * HEAD_DIM: base + (g + 1) * HEAD_DIM] = o[g * tq:(g + 1) * tq].astype(BF16)


def _sink_column(sink_ref, kvh, tq):
    grp = lax.broadcasted_iota(jnp.int32, (GROUP * tq, 1), 0) // tq
    col = jnp.full((GROUP * tq, 1), sink_ref[kvh * GROUP], F32)
    for g in range(1, GROUP):
        col = jnp.where(grp == g, sink_ref[kvh * GROUP + g], col)
    return col * LOG2_E


def _softmax_pv(scores, values, sink_col=None):
    m = scores[0].max(axis=-1, keepdims=True)
    for s in scores[1:]:
        m = jnp.maximum(m, s.max(axis=-1, keepdims=True))
    if sink_col is not None:
        m = jnp.maximum(m, sink_col)
    den = None
    acc = None
    for s, v in zip(scores, values):
        p = jnp.exp2(s - m)
        d = p.sum(axis=-1, keepdims=True)
        a = _dot(p.astype(BF16), v)
        den = d if den is None else den + d
        acc = a if acc is None else acc + a
    if sink_col is not None:
        den = den + jnp.exp2(sink_col - m)
    return acc / den


def _values_with_ones(v):
    return jnp.concatenate([v.T, jnp.ones((BF16_SUBLANES, v.shape[0]), BF16)], axis=0)


def _attn_a_kernel(tq, tk, q_ref, kc_ref, vc_ref, k_ref, v_ref, o_ref, acc_scr):
    q = _stack_heads(q_ref, 0)
    nc = k_ref.shape[0] // tk

    def kv(c):
        if c < 0:
            return kc_ref[...], vc_ref[...]
        return k_ref[c * tk:(c + 1) * tk, :], v_ref[c * tk:(c + 1) * tk, :]

    m = None
    k, v = kv(-1)
    s = _dot_nt(k, q)
    for c in range(-1, nc):
        if c + 1 < nc:
            k_next, v_next = kv(c + 1)
            s_next = _dot_nt(k_next, q)
        m_new = s.max(axis=0, keepdims=True)
        if m is not None:
            m_new = jnp.maximum(m, m_new)
        pv = _dot(_values_with_ones(v), jnp.exp2(s - m_new).astype(BF16))
        if m is None:
            acc_scr[...] = pv
        else:
            acc_scr[...] = jnp.exp2(m - m_new) * acc_scr[...] + pv
        m = m_new
        if c + 1 < nc:
            s, v = s_next, v_next
    acc = acc_scr[...]
    o = acc[:HEAD_DIM] / acc[HEAD_DIM:HEAD_DIM + 1]
    for g in range(GROUP):
        o_ref[:, g * HEAD_DIM:(g + 1) * HEAD_DIM] = o[:, g * tq:(g + 1) * tq].T.astype(BF16)


def _attn_a(qa, kv, kv_ctx, batch, seq, ctx_len, tq=256, tk=512):
    nq = seq // tq
    gw = GROUP * HEAD_DIM
    return pl.pallas_call(
        functools.partial(_attn_a_kernel, tq, tk),
        out_shape=jax.ShapeDtypeStruct(qa.shape, BF16),
        grid=(batch, A_KV_HEADS, nq),
        in_specs=[
            pl.BlockSpec((tq, gw), lambda b, h, i: (b * nq + i, h)),
            pl.BlockSpec((ctx_len, HEAD_DIM), lambda b, h, i: (b, h)),
            pl.BlockSpec((ctx_len, HEAD_DIM), lambda b, h, i: (b, A_KV_HEADS + h)),
            pl.BlockSpec((seq, HEAD_DIM), lambda b, h, i: (b, h)),
            pl.BlockSpec((seq, HEAD_DIM), lambda b, h, i: (b, A_KV_HEADS + h)),
        ],
        out_specs=pl.BlockSpec((tq, gw), lambda b, h, i: (b * nq + i, h)),
        scratch_shapes=[pltpu.VMEM((HEAD_DIM + BF16_SUBLANES, GROUP * tq), F32)],
        compiler_params=_params(("parallel", "parallel", "arbitrary")),
        name="attn_a",
    )(qa, kv_ctx, kv_ctx, kv, kv)


def _attn_b_kernel(seq, sink_ref, q_ref, kvc_ref, kv_ref, o_ref):
    i = pl.program_id(1)
    span = Q_BLOCK + 2 * WINDOW
    start = pl.multiple_of(jnp.clip(i * Q_BLOCK - WINDOW, 0, seq - span), Q_BLOCK)
    rows = GROUP * Q_BLOCK
    qpos = i * Q_BLOCK + lax.broadcasted_iota(jnp.int32, (rows, span), 0) % Q_BLOCK
    kpos = start + lax.broadcasted_iota(jnp.int32, (rows, span), 1)
    valid = jnp.abs(qpos - kpos) <= WINDOW
    for kvh in range(B_KV_HEADS):
        kc = slice(kvh * HEAD_DIM, (kvh + 1) * HEAD_DIM)
        vc = slice((B_KV_HEADS + kvh) * HEAD_DIM, (B_KV_HEADS + kvh + 1) * HEAD_DIM)
        q = _stack_heads(q_ref, kvh)
        s_win = jnp.where(valid, _dot_nt(q, kv_ref[pl.ds(start, span), kc]), NEG_INF)
        s_ctx = _dot_nt(q, kvc_ref[:, kc])
        o = _softmax_pv([s_win, s_ctx], [kv_ref[pl.ds(start, span), vc], kvc_ref[:, vc]],
                        _sink_column(sink_ref, kvh, Q_BLOCK))
        _unstack_heads(o_ref, kvh, o, Q_BLOCK)


def _attn_b(qb, kv, kv_ctx, sink, batch, seq, ctx_len):
    nq = seq // Q_BLOCK
    bw = 2 * B_KV_HEADS * HEAD_DIM
    return pl.pallas_call(
        functools.partial(_attn_b_kernel, seq),
        out_shape=jax.ShapeDtypeStruct(qb.shape, BF16),
        grid_spec=pltpu.PrefetchScalarGridSpec(
            num_scalar_prefetch=1,
            grid=(batch, nq),
            in_specs=[
                pl.BlockSpec((Q_BLOCK, Q_COLS), lambda b, i, s: (b * nq + i, 0)),
                pl.BlockSpec((ctx_len, bw), lambda b, i, s: (b, 1)),
                pl.BlockSpec((seq, bw), lambda b, i, s: (b, 1)),
            ],
            out_specs=pl.BlockSpec((Q_BLOCK, Q_COLS), lambda b, i, s: (b * nq + i, 0)),
        ),
        compiler_params=_params(("parallel", "arbitrary")),
        name="attn_b",
    )(sink, qb, kv_ctx, kv)


def _attn_ctx_kernel(ctx_len, sink_ref, qa_ref, qb_ref, kv_ref, oa_ref, ob_ref):
    for kvh in range(A_KV_HEADS):
        k = kv_ref[:, kvh * HEAD_DIM:(kvh + 1) * HEAD_DIM]
        v = kv_ref[:, (A_KV_HEADS + kvh) * HEAD_DIM:(A_KV_HEADS + kvh + 1) * HEAD_DIM]
        o = _softmax_pv([_dot_nt(_stack_heads(qa_ref, kvh), k)], [v])
        _unstack_heads(oa_ref, kvh, o, ctx_len)
    off = 2 * A_KV_HEADS
    for kvh in range(B_KV_HEADS):
        k = kv_ref[:, (off + kvh) * HEAD_DIM:(off + kvh + 1) * HEAD_DIM]
        v = kv_ref[:, (off + B_KV_HEADS + kvh) * HEAD_DIM:(off + B_KV_HEADS + kvh + 1) * HEAD_DIM]
        o = _softmax_pv([_dot_nt(_stack_heads(qb_ref, kvh), k)], [v], _sink_column(sink_ref, kvh, ctx_len))
        _unstack_heads(ob_ref, kvh, o, ctx_len)


def _attn_ctx(qa, qb, kv_ctx, sink, batch, ctx_len):
    q_spec = pl.BlockSpec((ctx_len, Q_COLS), lambda b, s: (b, 0))
    return pl.pallas_call(
        functools.partial(_attn_ctx_kernel, ctx_len),
        out_shape=[jax.ShapeDtypeStruct(qa.shape, BF16), jax.ShapeDtypeStruct(qb.shape, BF16)],
        grid_spec=pltpu.PrefetchScalarGridSpec(
            num_scalar_prefetch=1,
            grid=(batch,),
            in_specs=[q_spec, q_spec, pl.BlockSpec((ctx_len, KV_COLS), lambda b, s: (b, 0))],
            out_specs=[q_spec, q_spec],
        ),
        compiler_params=_params(("parallel",)),
        name="attn_ctx",
    )(sink, qa, qb, kv_ctx)


def _outproj_kernel(tm, blocks_per_seq, oa_ref, ob_ref, z_ref, zprev_ref, znext_ref, gp_ref, cw_ref, w_ref,
                    x_ref, gt_ref, gpost_ref, gffn_ref, sc_ref, sh_ref, xo_ref, h2_ref, oc_scr):
    i = pl.program_id(0)
    pos = i % blocks_per_seq
    z = z_ref[...].astype(F32)
    prev_row = jnp.where(pos == 0, 0.0, zprev_ref[BF16_SUBLANES - 1:BF16_SUBLANES, :].astype(F32))
    next_row = jnp.where(pos == blocks_per_seq - 1, 0.0, znext_ref[0:1, :].astype(F32))
    row = lax.broadcasted_iota(jnp.int32, z.shape, 0)
    z_prev = jnp.where(row == 0, prev_row, pltpu.roll(z, 1, 0))
    z_next = jnp.where(row == tm - 1, next_row, pltpu.roll(z, tm - 1, 0))
    cw = cw_ref[...]
    oc = gp_ref[...].astype(F32) * (z_prev * cw[0:1] + z * cw[1:2] + z_next * cw[2:3])
    oc_scr[...] = oc.astype(BF16)

    post_gain = gt_ref[...] * gpost_ref[...]
    ffn_gain = gffn_ref[...] * (1.0 + sc_ref[...])
    shift = sh_ref[...]
    sub = min(tm, ROW_SUB)
    for r in range(tm // sub):
        rows = slice(r * sub, (r + 1) * sub)
        y = _dot(oa_ref[rows, :], w_ref[0:Q_COLS, :])
        y += _dot(ob_ref[rows, :], w_ref[Q_COLS:2 * Q_COLS, :])
        y += _dot(oc_scr[rows, :], w_ref[2 * Q_COLS:, :])
        x = x_ref[rows, :] + _rms(y) * post_gain
        xo_ref[rows, :] = x
        h2_ref[rows, :] = (_rms(x) * ffn_gain + shift).astype(BF16)


def _outproj(oa, ob, z, gp, conv_w, w_out, layer, x, mod, row_of_block, g_post, g_ffn, tm, seq_len):
    m = x.shape[0]
    blocks_per_seq = seq_len // tm
    halo = tm // BF16_SUBLANES
    last_halo = m // BF16_SUBLANES - 1
    row_spec = lambda w: pl.BlockSpec((tm, w), lambda i: (i, 0))
    return pl.pallas_call(
        functools.partial(_outproj_kernel, tm, blocks_per_seq),
        out_shape=[jax.ShapeDtypeStruct((m, D_MODEL), F32), jax.ShapeDtypeStruct((m, D_MODEL), BF16)],
        grid=(m // tm,),
        in_specs=[
            row_spec(Q_COLS),
            row_spec(Q_COLS),
            row_spec(CONV_CH),
            pl.BlockSpec((BF16_SUBLANES, CONV_CH), lambda i: (jnp.maximum(i * halo - 1, 0), 0)),
            pl.BlockSpec((BF16_SUBLANES, CONV_CH), lambda i: (jnp.minimum((i + 1) * halo, last_halo), 0)),
            row_spec(CONV_CH),
            pl.BlockSpec(conv_w.shape, lambda i: (0, 0)),
            _resident((None, D_MODEL, D_MODEL), lambda i: (layer, 0, 0)),
            row_spec(D_MODEL),
            _mod_spec(2, row_of_block),
            _vec_spec(D_MODEL),
            _vec_spec(D_MODEL),
            _mod_spec(4, row_of_block),
            _mod_spec(3, row_of_block),
        ],
        out_specs=[row_spec(D_MODEL), row_spec(D_MODEL)],
        scratch_shapes=[pltpu.VMEM((tm, CONV_CH), BF16)],
        compiler_params=_params(("parallel",)),
        name="outproj",
    )(oa, ob, z, z, z, gp, conv_w, w_out, x, mod, g_post, g_ffn, mod, mod)


def _ffn_kernel(h_ref, wg_ref, wu_ref, wd_ref, x_ref, gt_ref, g_ref, o_ref, acc_ref):
    f = pl.program_id(1)

    @pl.when(f == 0)
    def _():
        acc_ref[...] = jnp.zeros_like(acc_ref)

    h = h_ref[...]
    a = (_silu(_dot(h, wg_ref[...])) * _dot(h, wu_ref[...])).astype(BF16)
    acc_ref[...] += _dot(a, wd_ref[...])

    @pl.when(f == pl.num_programs(1) - 1)
    def _():
        o_ref[...] = x_ref[...] + _rms(acc_ref[...]) * (gt_ref[...] * g_ref[...])


def _ffn(h2, w_gate, w_up, w_down, layer, x, mod, row_of_block, g_post, tm, tf=512):
    m = x.shape[0]
    hidden = w_gate.shape[-1]
    mod_spec = pl.BlockSpec((None, 1, D_MODEL), lambda i, f: (row_of_block(i) * N_MOD + 5, 0, 0))
    return pl.pallas_call(
        _ffn_kernel,
        out_shape=jax.ShapeDtypeStruct((m, D_MODEL), F32),
        grid=(m // tm, hidden // tf),
        in_specs=[
            pl.BlockSpec((tm, D_MODEL), lambda i, f: (i, 0)),
            pl.BlockSpec((None, D_MODEL, tf), lambda i, f: (layer, 0, f)),
            pl.BlockSpec((None, D_MODEL, tf), lambda i, f: (layer, 0, f)),
            pl.BlockSpec((None, tf, D_MODEL), lambda i, f: (layer, f, 0)),
            pl.BlockSpec((tm, D_MODEL), lambda i, f: (i, 0)),
            mod_spec,
            pl.BlockSpec((1, D_MODEL), lambda i, f: (0, 0)),
        ],
        out_specs=pl.BlockSpec((tm, D_MODEL), lambda i, f: (i, 0)),
        scratch_shapes=[pltpu.VMEM((tm, D_MODEL), F32)],
        compiler_params=_params(("parallel", "arbitrary")),
        name="ffn",
    )(h2, w_gate, w_up, w_down, x, mod, g_post)


def _rope_tables(seq):
    pos = jnp.arange(seq, dtype=jnp.int32)
    inv_freq = jnp.power(ROPE_BASE, -jnp.arange(ROPE_PAIRS, dtype=F32) / ROPE_PAIRS)
    ang = jnp.stack([pos // GRID_W, pos % GRID_W], axis=1).astype(F32)[:, :, None] * inv_freq
    cos, sin = jnp.cos(ang), jnp.sin(ang)
    cos_t = jnp.concatenate([cos[:, 0], cos[:, 0], cos[:, 1], cos[:, 1]], axis=-1)
    sin_t = jnp.concatenate([-sin[:, 0], sin[:, 0], -sin[:, 1], sin[:, 1]], axis=-1)
    return cos_t, sin_t


def kernel(x, c, ctx, c_ctx, w_ada, b_ada, g_mix_pre, g_mix_post, g_ffn_pre, g_ffn_post, w_in, q_norm, k_norm,
           sink, conv_w, w_out, w_gate, w_up, w_down):
    batch, seq, d = x.shape
    ctx_len = ctx.shape[1]
    depth = w_ada.shape[0]
    assert d == D_MODEL and batch < MOD_ROWS and seq % 512 == 0 and ctx_len % Q_BLOCK == 0

    tm = 512
    tm_ctx = ctx_len
    tm_ctx_ffn = min(512, batch * ctx_len)
    lat_row = lambda i: i // (seq // tm)
    ctx_row = lambda i: batch

    cond = jnp.zeros((MOD_ROWS, d), F32).at[:batch].set(c).at[batch].set(c_ctx)
    mod_all = _ada(cond, w_ada, b_ada).reshape(depth, MOD_ROWS * N_MOD, 1, d)
    tables = _rope_tables(seq)
    w_in, w_out = w_in.astype(BF16), w_out.astype(BF16)
    w_gate, w_up, w_down = w_gate.astype(BF16), w_up.astype(BF16), w_down.astype(BF16)

    xl = x.reshape(batch * seq, d)
    xc = ctx.reshape(batch * ctx_len, d)
    vec = lambda a, l: a[l].reshape(1, -1)
    for l in range(depth):
        mod = mod_all[l]
        g_pre, g_post = vec(g_mix_pre, l), vec(g_mix_post, l)
        g_ffn, g_ffn_out = vec(g_ffn_pre, l), vec(g_ffn_post, l)
        qn, kn = vec(q_norm, l), vec(k_norm, l)
        update_ctx = l < depth - 1

        kv_c, qa_c, qb_c, z_c, gp_c = _inproj(xc, mod, ctx_row, g_pre, w_in, l, qn, kn, None, tm_ctx, ctx_len)
        kv, qa, qb, z, gp = _inproj(xl, mod, lat_row, g_pre, w_in, l, qn, kn, tables, tm, seq)
        oa = _attn_a(qa, kv, kv_c, batch, seq, ctx_len)
        ob = _attn_b(qb, kv, kv_c, sink[l], batch, seq, ctx_len)
        xl, h2 = _outproj(oa, ob, z, gp, conv_w[l], w_out, l, xl, mod, lat_row, g_post, g_ffn, tm, seq)
        xl = _ffn(h2, w_gate, w_up, w_down, l, xl, mod, lat_row, g_ffn_out, tm)
        if update_ctx:
            oa_c, ob_c = _attn_ctx(qa_c, qb_c, kv_c, sink[l], batch, ctx_len)
            xc, h2_c = _outproj(oa_c, ob_c, z_c, gp_c, conv_w[l], w_out, l, xc, mod, ctx_row, g_post, g_ffn,
                                tm_ctx, ctx_len)
            xc = _ffn(h2_c, w_gate, w_up, w_down, l, xc, mod, ctx_row, g_ffn_out, tm_ctx_ffn)
    return xl.reshape(batch, seq, d)
```

```python
import functools

import jax
import jax.numpy as jnp
from jax import lax
from jax.experimental import pallas as pl
from jax.experimental.pallas import tpu as pltpu

F32 = jnp.float32
BF16 = jnp.bfloat16

D_MODEL = 2048
HEAD_DIM = 128
GRID_W = 64
A_Q_HEADS = 6
A_KV_HEADS = 2
B_Q_HEADS = 6
B_KV_HEADS = 2
GROUP = A_Q_HEADS // A_KV_HEADS
Q_COLS = A_Q_HEADS * HEAD_DIM
CONV_CH = D_MODEL - 2 * Q_COLS
WINDOW = 128
Q_BLOCK = 128
ROPE_BASE = 10000.0
ROPE_PAIRS = HEAD_DIM // 4
NORM_EPS = 1e-6
NEG_INF = -1e30
ATTN_SCALE = HEAD_DIM ** -0.5
LOG2_E = 1.4426950408889634
SCORE_SCALE = ATTN_SCALE * LOG2_E
N_MOD = 6
MOD_ROWS = 8

KV_COLS = 4 * A_KV_HEADS * HEAD_DIM
OFF_AQ = KV_COLS
OFF_BQ = OFF_AQ + Q_COLS
OFF_CONV = OFF_BQ + Q_COLS
IN_COLS = OFF_CONV + 3 * CONV_CH

VMEM_LIMIT = 56 * 1024 * 1024
BF16_SUBLANES = 16
ROW_SUB = 256


def _rms(t):
    return t * lax.rsqrt(jnp.mean(t * t, axis=-1, keepdims=True) + NORM_EPS)


def _silu(t):
    return t * (1.0 / (1.0 + jnp.exp(-t)))


def _dot(a, b):
    return jnp.dot(a, b, preferred_element_type=F32)


def _dot_nt(a, b):
    return lax.dot_general(a, b, (((1,), (1,)), ((), ())), preferred_element_type=F32)


def _params(semantics):
    return pltpu.CompilerParams(dimension_semantics=semantics, vmem_limit_bytes=VMEM_LIMIT)


def _resident(shape, index_map):
    return pl.BlockSpec(shape, index_map, pipeline_mode=pl.Buffered(1))


def _ada_kernel(s_ref, w_ref, b_ref, o_ref):
    a = _silu(s_ref[...]).astype(BF16)
    o_ref[0] = _dot(a, w_ref[0].astype(BF16)) + b_ref[0]


def _ada(cond, w_ada, b_ada, tn=1024):
    depth, d, n = w_ada.shape
    return pl.pallas_call(
        _ada_kernel,
        out_shape=jax.ShapeDtypeStruct((depth, MOD_ROWS, n), F32),
        grid=(depth, n // tn),
        in_specs=[
            pl.BlockSpec((MOD_ROWS, d), lambda l, j: (0, 0)),
            pl.BlockSpec((1, d, tn), lambda l, j: (l, 0, j)),
            pl.BlockSpec((1, 1, tn), lambda l, j: (l, 0, j)),
        ],
        out_specs=pl.BlockSpec((1, MOD_ROWS, tn), lambda l, j: (l, 0, j)),
        compiler_params=_params(("parallel", "parallel")),
        name="ada",
    )(cond, w_ada, b_ada.reshape(depth, 1, n))


def _mod_spec(chunk, row_of_block):
    return pl.BlockSpec((None, 1, D_MODEL), lambda i: (row_of_block(i) * N_MOD + chunk, 0, 0))


def _vec_spec(n):
    return pl.BlockSpec((1, n), lambda i: (0, 0))


def _inproj_kernel(rope, tm, x_ref, g_ref, sc_ref, sh_ref, w_ref, qn_ref, kn_ref, *rest):
    if rope:
        cos_ref, sin_ref, kv_ref, qa_ref, qb_ref, z_ref, gp_ref = rest
    else:
        kv_ref, qa_ref, qb_ref, z_ref, gp_ref = rest
    sub = min(tm, ROW_SUB)
    pre_gain = g_ref[...] * (1.0 + sc_ref[...])
    shift = sh_ref[...]
    k_gain = kn_ref[...] * SCORE_SCALE
    q_gain = qn_ref[...]
    lane = lax.broadcasted_iota(jnp.int32, (sub, HEAD_DIM), 1)
    first_half = (lane & ROPE_PAIRS) == 0

    for r in range(tm // sub):
        rows = slice(r * sub, (r + 1) * sub)
        if rope:
            cos = cos_ref[rows, :]
            sin = sin_ref[rows, :]

            def rot(t):
                partner = jnp.where(first_half, pltpu.roll(t, HEAD_DIM - ROPE_PAIRS, 1),
                                    pltpu.roll(t, ROPE_PAIRS, 1))
                return t * cos + partner * sin
        else:
            def rot(t):
                return t

        hb = (_rms(x_ref[rows, :]) * pre_gain + shift).astype(BF16)

        def proj(c0, n):
            return _dot(hb, w_ref[:, c0:c0 + n])

        def head(t, k):
            return t[:, k * HEAD_DIM:(k + 1) * HEAD_DIM]

        def put(ref, k, t):
            ref[rows, k * HEAD_DIM:(k + 1) * HEAD_DIM] = t.astype(BF16)

        pkv = proj(0, KV_COLS)
        for k in range(A_KV_HEADS):
            put(kv_ref, k, rot(_rms(head(pkv, k)) * k_gain))
            put(kv_ref, A_KV_HEADS + k, head(pkv, A_KV_HEADS + k))
            put(kv_ref, 2 * A_KV_HEADS + k, rot(head(pkv, 2 * A_KV_HEADS + k)) * SCORE_SCALE)
            put(kv_ref, 3 * A_KV_HEADS + k, head(pkv, 3 * A_KV_HEADS + k))

        pq = proj(OFF_AQ, Q_COLS)
        for k in range(A_Q_HEADS):
            put(qa_ref, k, rot(_rms(head(pq, k)) * q_gain))
        pq = proj(OFF_BQ, Q_COLS)
        for k in range(B_Q_HEADS):
            put(qb_ref, k, rot(head(pq, k)))

        z_ref[rows, :] = (proj(OFF_CONV, CONV_CH) * proj(OFF_CONV + 2 * CONV_CH, CONV_CH)).astype(BF16)
        gp_ref[rows, :] = proj(OFF_CONV + CONV_CH, CONV_CH).astype(BF16)


def _inproj(x, mod, row_of_block, g_pre, w_in, layer, q_norm, k_norm, rope_tables, tm, seq_len):
    m = x.shape[0]
    rope = rope_tables is not None
    in_specs = [
        pl.BlockSpec((tm, D_MODEL), lambda i: (i, 0)),
        _vec_spec(D_MODEL),
        _mod_spec(1, row_of_block),
        _mod_spec(0, row_of_block),
        _resident((None, D_MODEL, IN_COLS), lambda i: (layer, 0, 0)),
        _vec_spec(HEAD_DIM),
        _vec_spec(HEAD_DIM),
    ]
    args = [x, g_pre, mod, mod, w_in, q_norm, k_norm]
    if rope:
        blocks_per_seq = seq_len // tm
        in_specs += [pl.BlockSpec((tm, HEAD_DIM), lambda i: (i % blocks_per_seq, 0))] * 2
        args += list(rope_tables)
    widths = (KV_COLS, Q_COLS, Q_COLS, CONV_CH, CONV_CH)
    return pl.pallas_call(
        functools.partial(_inproj_kernel, rope, tm),
        out_shape=[jax.ShapeDtypeStruct((m, w), BF16) for w in widths],
        grid=(m // tm,),
        in_specs=in_specs,
        out_specs=[pl.BlockSpec((tm, w), lambda i: (i, 0)) for w in widths],
        compiler_params=_params(("parallel",)),
        name="inproj_rope" if rope else "inproj",
    )(*args)


def _stack_heads(q_ref, kvh, rows=slice(None)):
    base = kvh * GROUP * HEAD_DIM
    return jnp.concatenate(
        [q_ref[rows, base + g * HEAD_DIM: base + (g + 1) * HEAD_DIM] for g in range(GROUP)], axis=0)


def _unstack_heads(o_ref, kvh, o, tq, rows=slice(None)):
    base = kvh * GROUP * HEAD_DIM
    for g in range(GROUP):
        o_ref[rows, base + g * HEAD_DIM: base + (g + 1) * HEAD_DIM] = o[g * tq:(g + 1) * tq].astype(BF16)


def _sink_column(sink_ref, kvh, tq):
    grp = lax.broadcasted_iota(jnp.int32, (GROUP * tq, 1), 0) // tq
    col = jnp.full((GROUP * tq, 1), sink_ref[kvh * GROUP], F32)
    for g in range(1, GROUP):
        col = jnp.where(grp == g, sink_ref[kvh * GROUP + g], col)
    return col * LOG2_E


def _softmax_pv(scores, values, sink_col=None):
    m = scores[0].max(axis=-1, keepdims=True)
    for s in scores[1:]:
        m = jnp.maximum(m, s.max(axis=-1, keepdims=True))
    if sink_col is not None:
        m = jnp.maximum(m, sink_col)
    den = None
    acc = None
    for s, v in zip(scores, values):
        p = jnp.exp2(s - m)
        d = p.sum(axis=-1, keepdims=True)
        a = _dot(p.astype(BF16), v)
        den = d if den is None else den + d
        acc = a if acc is None else acc + a
    if sink_col is not None:
        den = den + jnp.exp2(sink_col - m)
    return acc / den


def _values_with_ones(v):
    return jnp.concatenate([v.T, jnp.ones((BF16_SUBLANES, v.shape[0]), BF16)], axis=0)


def _attn_a_kernel(tq, tk, q_ref, kc_ref, vc_ref, k_ref, v_ref, o_ref, acc_scr):
    q = _stack_heads(q_ref, 0)
    nc = k_ref.shape[0] // tk

    def kv(c):
        if c < 0:
            return kc_ref[...], vc_ref[...]
        return k_ref[c * tk:(c + 1) * tk, :], v_ref[c * tk:(c + 1) * tk, :]

    m = None
    k, v = kv(-1)
    s = _dot_nt(k, q)
    for c in range(-1, nc):
        if c + 1 < nc:
            k_next, v_next = kv(c + 1)
            s_next = _dot_nt(k_next, q)
        m_new = s.max(axis=0, keepdims=True)
        if m is not None:
            m_new = jnp.maximum(m, m_new)
        pv = _dot(_values_with_ones(v), jnp.exp2(s - m_new).astype(BF16))
        if m is None:
            acc_scr[...] = pv
        else:
            acc_scr[...] = jnp.exp2(m - m_new) * acc_scr[...] + pv
        m = m_new
        if c + 1 < nc:
            s, v = s_next, v_next
    acc = acc_scr[...]
    o = acc[:HEAD_DIM] / acc[HEAD_DIM:HEAD_DIM + 1]
    for g in range(GROUP):
        o_ref[:, g * HEAD_DIM:(g + 1) * HEAD_DIM] = o[:, g * tq:(g + 1) * tq].T.astype(BF16)


def _attn_a(qa, kv, kv_ctx, batch, seq, ctx_len, tq=256, tk=512):
    nq = seq // tq
    gw = GROUP * HEAD_DIM
    return pl.pallas_call(
        functools.partial(_attn_a_kernel, tq, tk),
        out_shape=jax.ShapeDtypeStruct(qa.shape, BF16),
        grid=(batch, A_KV_HEADS, nq),
        in_specs=[
            pl.BlockSpec((tq, gw), lambda b, h, i: (b * nq + i, h)),
            pl.BlockSpec((ctx_len, HEAD_DIM), lambda b, h, i: (b, h)),
            pl.BlockSpec((ctx_len, HEAD_DIM), lambda b, h, i: (b, A_KV_HEADS + h)),
            pl.BlockSpec((seq, HEAD_DIM), lambda b, h, i: (b, h)),
            pl.BlockSpec((seq, HEAD_DIM), lambda b, h, i: (b, A_KV_HEADS + h)),
        ],
        out_specs=pl.BlockSpec((tq, gw), lambda b, h, i: (b * nq + i, h)),
        scratch_shapes=[pltpu.VMEM((HEAD_DIM + BF16_SUBLANES, GROUP * tq), F32)],
        compiler_params=_params(("parallel", "parallel", "arbitrary")),
        name="attn_a",
    )(qa, kv_ctx, kv_ctx, kv, kv)


def _attn_b_kernel(seq, nblk, sink_ref, q_ref, kvc_ref, kv_ref, o_ref):
    span = Q_BLOCK + 2 * WINDOW
    rows = GROUP * Q_BLOCK
    row_q = lax.broadcasted_iota(jnp.int32, (rows, span), 0) % Q_BLOCK
    col_k = lax.broadcasted_iota(jnp.int32, (rows, span), 1)
    work = []
    for j in range(nblk):
        blk = pl.program_id(1) * nblk + j
        qrows = slice(j * Q_BLOCK, (j + 1) * Q_BLOCK)
        start = pl.multiple_of(jnp.clip(blk * Q_BLOCK - WINDOW, 0, seq - span), Q_BLOCK)
        valid = jnp.abs(blk * Q_BLOCK - start + row_q - col_k) <= WINDOW
        for kvh in range(B_KV_HEADS):
            kc = slice(kvh * HEAD_DIM, (kvh + 1) * HEAD_DIM)
            q = _stack_heads(q_ref, kvh, qrows)
            s_win = jnp.where(valid, _dot_nt(q, kv_ref[pl.ds(start, span), kc]), NEG_INF)
            s_ctx = _dot_nt(q, kvc_ref[:, kc])
            work.append((qrows, kvh, start, s_win, s_ctx))
    for qrows, kvh, start, s_win, s_ctx in work:
        vc = slice((B_KV_HEADS + kvh) * HEAD_DIM, (B_KV_HEADS + kvh + 1) * HEAD_DIM)
        o = _softmax_pv([s_win, s_ctx], [kv_ref[pl.ds(start, span), vc], kvc_ref[:, vc]],
                        _sink_column(sink_ref, kvh, Q_BLOCK))
        _unstack_heads(o_ref, kvh, o, Q_BLOCK, qrows)


def _attn_b(qb, kv, kv_ctx, sink, batch, seq, ctx_len, nblk=4):
    nq = seq // (Q_BLOCK * nblk)
    bw = 2 * B_KV_HEADS * HEAD_DIM
    return pl.pallas_call(
        functools.partial(_attn_b_kernel, seq, nblk),
        out_shape=jax.ShapeDtypeStruct(qb.shape, BF16),
        grid_spec=pltpu.PrefetchScalarGridSpec(
            num_scalar_prefetch=1,
            grid=(batch, nq),
            in_specs=[
                pl.BlockSpec((Q_BLOCK * nblk, Q_COLS), lambda b, i, s: (b * nq + i, 0)),
                pl.BlockSpec((ctx_len, bw), lambda b, i, s: (b, 1)),
                pl.BlockSpec((seq, bw), lambda b, i, s: (b, 1)),
            ],
            out_specs=pl.BlockSpec((Q_BLOCK * nblk, Q_COLS), lambda b, i, s: (b * nq + i, 0)),
        ),
        compiler_params=_params(("parallel", "arbitrary")),
        name="attn_b",
    )(sink, qb, kv_ctx, kv)


def _attn_ctx_kernel(ctx_len, sink_ref, qa_ref, qb_ref, kv_ref, oa_ref, ob_ref):
    for kvh in range(A_KV_HEADS):
        k = kv_ref[:, kvh * HEAD_DIM:(kvh + 1) * HEAD_DIM]
        v = kv_ref[:, (A_KV_HEADS + kvh) * HEAD_DIM:(A_KV_HEADS + kvh + 1) * HEAD_DIM]
        o = _softmax_pv([_dot_nt(_stack_heads(qa_ref, kvh), k)], [v])
        _unstack_heads(oa_ref, kvh, o, ctx_len)
    off = 2 * A_KV_HEADS
    for kvh in range(B_KV_HEADS):
        k = kv_ref[:, (off + kvh) * HEAD_DIM:(off + kvh + 1) * HEAD_DIM]
        v = kv_ref[:, (off + B_KV_HEADS + kvh) * HEAD_DIM:(off + B_KV_HEADS + kvh + 1) * HEAD_DIM]
        o = _softmax_pv([_dot_nt(_stack_heads(qb_ref, kvh), k)], [v], _sink_column(sink_ref, kvh, ctx_len))
        _unstack_heads(ob_ref, kvh, o, ctx_len)


def _attn_ctx(qa, qb, kv_ctx, sink, batch, ctx_len):
    q_spec = pl.BlockSpec((ctx_len, Q_COLS), lambda b, s: (b, 0))
    return pl.pallas_call(
        functools.partial(_attn_ctx_kernel, ctx_len),
        out_shape=[jax.ShapeDtypeStruct(qa.shape, BF16), jax.ShapeDtypeStruct(qb.shape, BF16)],
        grid_spec=pltpu.PrefetchScalarGridSpec(
            num_scalar_prefetch=1,
            grid=(batch,),
            in_specs=[q_spec, q_spec, pl.BlockSpec((ctx_len, KV_COLS), lambda b, s: (b, 0))],
            out_specs=[q_spec, q_spec],
        ),
        compiler_params=_params(("parallel",)),
        name="attn_ctx",
    )(sink, qa, qb, kv_ctx)


def _outproj_kernel(tm, blocks_per_seq, oa_ref, ob_ref, z_ref, zprev_ref, znext_ref, gp_ref, cw_ref, w_ref,
                    x_ref, gt_ref, gpost_ref, gffn_ref, sc_ref, sh_ref, xo_ref, h2_ref, oc_scr):
    i = pl.program_id(0)
    pos = i % blocks_per_seq
    z = z_ref[...].astype(F32)
    prev_row = jnp.where(pos == 0, 0.0, zprev_ref[BF16_SUBLANES - 1:BF16_SUBLANES, :].astype(F32))
    next_row = jnp.where(pos == blocks_per_seq - 1, 0.0, znext_ref[0:1, :].astype(F32))
    row = lax.broadcasted_iota(jnp.int32, z.shape, 0)
    z_prev = jnp.where(row == 0, prev_row, pltpu.roll(z, 1, 0))
    z_next = jnp.where(row == tm - 1, next_row, pltpu.roll(z, tm - 1, 0))
    cw = cw_ref[...]
    oc = gp_ref[...].astype(F32) * (z_prev * cw[0:1] + z * cw[1:2] + z_next * cw[2:3])
    oc_scr[...] = oc.astype(BF16)

    post_gain = gt_ref[...] * gpost_ref[...]
    ffn_gain = gffn_ref[...] * (1.0 + sc_ref[...])
    shift = sh_ref[...]
    sub = min(tm, ROW_SUB)
    for r in range(tm // sub):
        rows = slice(r * sub, (r + 1) * sub)
        y = _dot(oa_ref[rows, :], w_ref[0:Q_COLS, :])
        y += _dot(ob_ref[rows, :], w_ref[Q_COLS:2 * Q_COLS, :])
        y += _dot(oc_scr[rows, :], w_ref[2 * Q_COLS:, :])
        x = x_ref[rows, :] + _rms(y) * post_gain
        xo_ref[rows, :] = x
        h2_ref[rows, :] = (_rms(x) * ffn_gain + shift).astype(BF16)


def _outproj(oa, ob, z, gp, conv_w, w_out, layer, x, mod, row_of_block, g_post, g_ffn, tm, seq_len):
    m = x.shape[0]
    blocks_per_seq = seq_len // tm
    halo = tm // BF16_SUBLANES
    last_halo = m // BF16_SUBLANES - 1
    row_spec = lambda w: pl.BlockSpec((tm, w), lambda i: (i, 0))
    return pl.pallas_call(
        functools.partial(_outproj_kernel, tm, blocks_per_seq),
        out_shape=[jax.ShapeDtypeStruct((m, D_MODEL), F32), jax.ShapeDtypeStruct((m, D_MODEL), BF16)],
        grid=(m // tm,),
        in_specs=[
            row_spec(Q_COLS),
            row_spec(Q_COLS),
            row_spec(CONV_CH),
            pl.BlockSpec((BF16_SUBLANES, CONV_CH), lambda i: (jnp.maximum(i * halo - 1, 0), 0)),
            pl.BlockSpec((BF16_SUBLANES, CONV_CH), lambda i: (jnp.minimum((i + 1) * halo, last_halo), 0)),
            row_spec(CONV_CH),
            pl.BlockSpec(conv_w.shape, lambda i: (0, 0)),
            _resident((None, D_MODEL, D_MODEL), lambda i: (layer, 0, 0)),
            row_spec(D_MODEL),
            _mod_spec(2, row_of_block),
            _vec_spec(D_MODEL),
            _vec_spec(D_MODEL),
            _mod_spec(4, row_of_block),
            _mod_spec(3, row_of_block),
        ],
        out_specs=[row_spec(D_MODEL), row_spec(D_MODEL)],
        scratch_shapes=[pltpu.VMEM((tm, CONV_CH), BF16)],
        compiler_params=_params(("parallel",)),
        name="outproj",
    )(oa, ob, z, z, z, gp, conv_w, w_out, x, mod, g_post, g_ffn, mod, mod)


def _ffn_up_kernel(tm, h_ref, wg_ref, wu_ref, a_ref):
    sub = min(tm, ROW_SUB)
    for r in range(tm // sub):
        rows = slice(r * sub, (r + 1) * sub)
        h = h_ref[rows, :]
        a_ref[rows, :] = (_silu(_dot(h, wg_ref[...])) * _dot(h, wu_ref[...])).astype(BF16)


def _ffn_down_kernel(tm, a_ref, wd_ref, x_ref, gt_ref, g_ref, o_ref):
    gain = gt_ref[...] * g_ref[...]
    sub = min(tm, ROW_SUB)
    for r in range(tm // sub):
        rows = slice(r * sub, (r + 1) * sub)
        o_ref[rows, :] = x_ref[rows, :] + _rms(_dot(a_ref[rows, :], wd_ref[...])) * gain


def _ffn(h2, w_gate, w_up, w_down, layer, x, mod, row_of_block, g_post, tm_up, tm_down, tf=512):
    m = x.shape[0]
    hidden = w_gate.shape[-1]
    act = pl.pallas_call(
        functools.partial(_ffn_up_kernel, tm_up),
        out_shape=jax.ShapeDtypeStruct((m, hidden), BF16),
        grid=(m // tm_up, hidden // tf),
        in_specs=[
            pl.BlockSpec((tm_up, D_MODEL), lambda i, f: (i, 0)),
            pl.BlockSpec((None, D_MODEL, tf), lambda i, f: (layer, 0, f)),
            pl.BlockSpec((None, D_MODEL, tf), lambda i, f: (layer, 0, f)),
        ],
        out_specs=pl.BlockSpec((tm_up, tf), lambda i, f: (i, f)),
        compiler_params=_params(("parallel", "arbitrary")),
        name="ffn_up",
    )(h2, w_gate, w_up)
    return pl.pallas_call(
        functools.partial(_ffn_down_kernel, tm_down),
        out_shape=jax.ShapeDtypeStruct((m, D_MODEL), F32),
        grid=(m // tm_down,),
        in_specs=[
            pl.BlockSpec((tm_down, hidden), lambda i: (i, 0)),
            _resident((None, hidden, D_MODEL), lambda i: (layer, 0, 0)),
            pl.BlockSpec((tm_down, D_MODEL), lambda i: (i, 0)),
            _mod_spec(5, row_of_block),
            _vec_spec(D_MODEL),
        ],
        out_specs=pl.BlockSpec((tm_down, D_MODEL), lambda i: (i, 0)),
        compiler_params=_params(("parallel",)),
        name="ffn_down",
    )(act, w_down, x, mod, g_post)


def _rope_tables(seq):
    pos = jnp.arange(seq, dtype=jnp.int32)
    inv_freq = jnp.power(ROPE_BASE, -jnp.arange(ROPE_PAIRS, dtype=F32) / ROPE_PAIRS)
    ang = jnp.stack([pos // GRID_W, pos % GRID_W], axis=1).astype(F32)[:, :, None] * inv_freq
    cos, sin = jnp.cos(ang), jnp.sin(ang)
    cos_t = jnp.concatenate([cos[:, 0], cos[:, 0], cos[:, 1], cos[:, 1]], axis=-1)
    sin_t = jnp.concatenate([-sin[:, 0], sin[:, 0], -sin[:, 1], sin[:, 1]], axis=-1)
    return cos_t, sin_t


def kernel(x, c, ctx, c_ctx, w_ada, b_ada, g_mix_pre, g_mix_post, g_ffn_pre, g_ffn_post, w_in, q_norm, k_norm,
           sink, conv_w, w_out, w_gate, w_up, w_down):
    batch, seq, d = x.shape
    ctx_len = ctx.shape[1]
    depth = w_ada.shape[0]
    assert d == D_MODEL and batch < MOD_ROWS and seq % 512 == 0 and ctx_len % Q_BLOCK == 0

    tm = 512
    tm_ctx = ctx_len
    tm_up = 2048
    tm_ctx_ffn = min(tm, batch * ctx_len)
    lat_row = lambda i: i // (seq // tm)
    ctx_row = lambda i: batch

    cond = jnp.zeros((MOD_ROWS, d), F32).at[:batch].set(c).at[batch].set(c_ctx)
    mod_all = _ada(cond, w_ada, b_ada).reshape(depth, MOD_ROWS * N_MOD, 1, d)
    tables = _rope_tables(seq)
    w_in, w_out = w_in.astype(BF16), w_out.astype(BF16)
    w_gate, w_up, w_down = w_gate.astype(BF16), w_up.astype(BF16), w_down.astype(BF16)

    xl = x.reshape(batch * seq, d)
    xc = ctx.reshape(batch * ctx_len, d)
    vec = lambda a, l: a[l].reshape(1, -1)
    for l in range(depth):
        mod = mod_all[l]
        g_pre, g_post = vec(g_mix_pre, l), vec(g_mix_post, l)
        g_ffn, g_ffn_out = vec(g_ffn_pre, l), vec(g_ffn_post, l)
        qn, kn = vec(q_norm, l), vec(k_norm, l)
        update_ctx = l < depth - 1

        kv_c, qa_c, qb_c, z_c, gp_c = _inproj(xc, mod, ctx_row, g_pre, w_in, l, qn, kn, None, tm_ctx, ctx_len)
        kv, qa, qb, z, gp = _inproj(xl, mod, lat_row, g_pre, w_in, l, qn, kn, tables, tm, seq)
        oa = _attn_a(qa, kv, kv_c, batch, seq, ctx_len)
        ob = _attn_b(qb, kv, kv_c, sink[l], batch, seq, ctx_len)
        xl, h2 = _outproj(oa, ob, z, gp, conv_w[l], w_out, l, xl, mod, lat_row, g_post, g_ffn, tm, seq)
        xl = _ffn(h2, w_gate, w_up, w_down, l, xl, mod, lat_row, g_ffn_out, tm_up, tm)
        if update_ctx:
            oa_c, ob_c = _attn_ctx(qa_c, qb_c, kv_c, sink[l], batch, ctx_len)
            xc, h2_c = _outproj(oa_c, ob_c, z_c, gp_c, conv_w[l], w_out, l, xc, mod, ctx_row, g_post, g_ffn,
                                tm_ctx, ctx_len)
            xc = _ffn(h2_c, w_gate, w_up, w_down, l, xc, mod, ctx_row, g_ffn_out,
                      min(tm_up, batch * ctx_len), tm_ctx_ffn)
    return xl.reshape(batch, seq, d)
```

```python
import functools

import jax
import jax.numpy as jnp
from jax import lax
from jax.experimental import pallas as pl
from jax.experimental.pallas import tpu as pltpu

F32 = jnp.float32
BF16 = jnp.bfloat16

D_MODEL = 2048
HEAD_DIM = 128
GRID_W = 64
A_Q_HEADS = 6
A_KV_HEADS = 2
B_Q_HEADS = 6
B_KV_HEADS = 2
GROUP = A_Q_HEADS // A_KV_HEADS
Q_COLS = A_Q_HEADS * HEAD_DIM
CONV_CH = D_MODEL - 2 * Q_COLS
WINDOW = 128
Q_BLOCK = 128
ROPE_BASE = 10000.0
ROPE_PAIRS = HEAD_DIM // 4
NORM_EPS = 1e-6
NEG_INF = -1e30
ATTN_SCALE = HEAD_DIM ** -0.5
LOG2_E = 1.4426950408889634
SCORE_SCALE = ATTN_SCALE * LOG2_E
N_MOD = 6
MOD_ROWS = 8

KV_COLS = 4 * A_KV_HEADS * HEAD_DIM
OFF_AQ = KV_COLS
OFF_BQ = OFF_AQ + Q_COLS
OFF_CONV = OFF_BQ + Q_COLS
IN_COLS = OFF_CONV + 3 * CONV_CH

VMEM_LIMIT = 56 * 1024 * 1024
BF16_SUBLANES = 16
ROW_SUB = 256


def _rms(t):
    return t * lax.rsqrt(jnp.mean(t * t, axis=-1, keepdims=True) + NORM_EPS)


def _silu(t):
    return t * (1.0 / (1.0 + jnp.exp(-t)))


def _dot(a, b):
    return jnp.dot(a, b, preferred_element_type=F32)


def _dot_nt(a, b):
    return lax.dot_general(a, b, (((1,), (1,)), ((), ())), preferred_element_type=F32)


def _params(semantics):
    return pltpu.CompilerParams(dimension_semantics=semantics, vmem_limit_bytes=VMEM_LIMIT)


def _resident(shape, index_map):
    return pl.BlockSpec(shape, index_map, pipeline_mode=pl.Buffered(1))


def _ada_kernel(s_ref, w_ref, b_ref, o_ref):
    a = _silu(s_ref[...]).astype(BF16)
    o_ref[0] = _dot(a, w_ref[0].astype(BF16)) + b_ref[0]


def _ada(cond, w_ada, b_ada, tn=1024):
    depth, d, n = w_ada.shape
    return pl.pallas_call(
        _ada_kernel,
        out_shape=jax.ShapeDtypeStruct((depth, MOD_ROWS, n), F32),
        grid=(depth, n // tn),
        in_specs=[
            pl.BlockSpec((MOD_ROWS, d), lambda l, j: (0, 0)),
            pl.BlockSpec((1, d, tn), lambda l, j: (l, 0, j)),
            pl.BlockSpec((1, 1, tn), lambda l, j: (l, 0, j)),
        ],
        out_specs=pl.BlockSpec((1, MOD_ROWS, tn), lambda l, j: (l, 0, j)),
        compiler_params=_params(("parallel", "parallel")),
        name="ada",
    )(cond, w_ada, b_ada.reshape(depth, 1, n))


def _mod_spec(chunk, row_of_block):
    return pl.BlockSpec((None, 1, D_MODEL), lambda i: (row_of_block(i) * N_MOD + chunk, 0, 0))


def _vec_spec(n):
    return pl.BlockSpec((1, n), lambda i: (0, 0))


def _inproj_kernel(rope, kv_only, tm, x_ref, g_ref, sc_ref, sh_ref, w_ref, qn_ref, kn_ref, *rest):
    if rope:
        cos_ref, sin_ref = rest[:2]
        rest = rest[2:]
    if kv_only:
        kv_ref, = rest
    else:
        kv_ref, qa_ref, qb_ref, z_ref, gp_ref = rest
    sub = min(tm, ROW_SUB)
    pre_gain = g_ref[...] * (1.0 + sc_ref[...])
    shift = sh_ref[...]
    k_gain = kn_ref[...] * SCORE_SCALE
    q_gain = qn_ref[...]
    lane = lax.broadcasted_iota(jnp.int32, (sub, HEAD_DIM), 1)
    first_half = (lane & ROPE_PAIRS) == 0

    for r in range(tm // sub):
        rows = slice(r * sub, (r + 1) * sub)
        if rope:
            cos = cos_ref[rows, :]
            sin = sin_ref[rows, :]

            def rot(t):
                partner = jnp.where(first_half, pltpu.roll(t, HEAD_DIM - ROPE_PAIRS, 1),
                                    pltpu.roll(t, ROPE_PAIRS, 1))
                return t * cos + partner * sin
        else:
            def rot(t):
                return t

        hb = (_rms(x_ref[rows, :]) * pre_gain + shift).astype(BF16)

        def proj(c0, n):
            return _dot(hb, w_ref[:, c0:c0 + n])

        def head(t, k):
            return t[:, k * HEAD_DIM:(k + 1) * HEAD_DIM]

        def put(ref, k, t):
            ref[rows, k * HEAD_DIM:(k + 1) * HEAD_DIM] = t.astype(BF16)

        pkv = proj(0, KV_COLS)
        for k in range(A_KV_HEADS):
            put(kv_ref, k, rot(_rms(head(pkv, k)) * k_gain))
            put(kv_ref, A_KV_HEADS + k, head(pkv, A_KV_HEADS + k))
            put(kv_ref, 2 * A_KV_HEADS + k, rot(head(pkv, 2 * A_KV_HEADS + k)) * SCORE_SCALE)
            put(kv_ref, 3 * A_KV_HEADS + k, head(pkv, 3 * A_KV_HEADS + k))
        if kv_only:
            continue

        pq = proj(OFF_AQ, Q_COLS)
        for k in range(A_Q_HEADS):
            put(qa_ref, k, rot(_rms(head(pq, k)) * q_gain))
        pq = proj(OFF_BQ, Q_COLS)
        for k in range(B_Q_HEADS):
            put(qb_ref, k, rot(head(pq, k)))

        z_ref[rows, :] = (proj(OFF_CONV, CONV_CH) * proj(OFF_CONV + 2 * CONV_CH, CONV_CH)).astype(BF16)
        gp_ref[rows, :] = proj(OFF_CONV + CONV_CH, CONV_CH).astype(BF16)


def _inproj(x, mod, row_of_block, g_pre, w_in, layer, q_norm, k_norm, rope_tables, tm, seq_len, kv_only=False):
    m = x.shape[0]
    rope = rope_tables is not None
    cols = KV_COLS if kv_only else IN_COLS
    in_specs = [
        pl.BlockSpec((tm, D_MODEL), lambda i: (i, 0)),
        _vec_spec(D_MODEL),
        _mod_spec(1, row_of_block),
        _mod_spec(0, row_of_block),
        _resident((None, D_MODEL, cols), lambda i: (layer, 0, 0)),
        _vec_spec(HEAD_DIM),
        _vec_spec(HEAD_DIM),
    ]
    args = [x, g_pre, mod, mod, w_in, q_norm, k_norm]
    if rope:
        blocks_per_seq = seq_len // tm
        in_specs += [pl.BlockSpec((tm, HEAD_DIM), lambda i: (i % blocks_per_seq, 0))] * 2
        args += list(rope_tables)
    widths = (KV_COLS,) if kv_only else (KV_COLS, Q_COLS, Q_COLS, CONV_CH, CONV_CH)
    return pl.pallas_call(
        functools.partial(_inproj_kernel, rope, kv_only, tm),
        out_shape=[jax.ShapeDtypeStruct((m, w), BF16) for w in widths],
        grid=(m // tm,),
        in_specs=in_specs,
        out_specs=[pl.BlockSpec((tm, w), lambda i: (i, 0)) for w in widths],
        compiler_params=_params(("parallel",)),
        name=("inproj_rope" if rope else "inproj") + ("_kv" if kv_only else ""),
    )(*args)


def _stack_heads(q_ref, kvh, rows=slice(None)):
    base = kvh * GROUP * HEAD_DIM
    return jnp.concatenate(
        [q_ref[rows, base + g * HEAD_DIM: base + (g + 1) * HEAD_DIM] for g in range(GROUP)], axis=0)


def _unstack_heads(o_ref, kvh, o, tq, rows=slice(None)):
    base = kvh * GROUP * HEAD_DIM
    for g in range(GROUP):
        o_ref[rows, base + g * HEAD_DIM: base + (g + 1) * HEAD_DIM] = o[g * tq:(g + 1) * tq].astype(BF16)


def _sink_column(sink_ref, kvh, tq):
    grp = lax.broadcasted_iota(jnp.int32, (GROUP * tq, 1), 0) // tq
    col = jnp.full((GROUP * tq, 1), sink_ref[kvh * GROUP], F32)
    for g in range(1, GROUP):
        col = jnp.where(grp == g, sink_ref[kvh * GROUP + g], col)
    return col * LOG2_E


def _softmax_pv(scores, values, sink_col=None):
    m = scores[0].max(axis=-1, keepdims=True)
    for s in scores[1:]:
        m = jnp.maximum(m, s.max(axis=-1, keepdims=True))
    if sink_col is not None:
        m = jnp.maximum(m, sink_col)
    den = None
    acc = None
    for s, v in zip(scores, values):
        p = jnp.exp2(s - m)
        d = p.sum(axis=-1, keepdims=True)
        a = _dot(p.astype(BF16), v)
        den = d if den is None else den + d
        acc = a if acc is None else acc + a
    if sink_col is not None:
        den = den + jnp.exp2(sink_col - m)
    return acc / den


def _values_with_ones(v):
    return jnp.concatenate([v.T, jnp.ones((BF16_SUBLANES, v.shape[0]), BF16)], axis=0)


def _attn_a_kernel(tq, tk, nblk, q_ref, kc_ref, vc_ref, k_ref, v_ref, o_ref, acc_scr):
    nc = k_ref.shape[0] // tk

    def scores(stage):
        j, c = stage
        k, v = (kc_ref[...], vc_ref[...]) if c < 0 else (k_ref[c * tk:(c + 1) * tk, :], v_ref[c * tk:(c + 1) * tk, :])
        return _dot_nt(k, qs[j]), v

    stages = [(j, c) for j in range(nblk) for c in range(-1, nc)]
    qs = [_stack_heads(q_ref, 0, slice(j * tq, (j + 1) * tq)) for j in range(nblk)]
    m = None
    s, v = scores(stages[0])
    for n, (j, c) in enumerate(stages):
        if n + 1 < len(stages):
            s_next, v_next = scores(stages[n + 1])
        acc = acc_scr.at[j]
        m_new = s.max(axis=0, keepdims=True)
        if c >= 0:
            m_new = jnp.maximum(m, m_new)
        pv = _dot(_values_with_ones(v), jnp.exp2(s - m_new).astype(BF16))
        if c < 0:
            acc[...] = pv
        else:
            acc[...] = jnp.exp2(m - m_new) * acc[...] + pv
        m = m_new
        if c == nc - 1:
            a = acc[...]
            o = a[:HEAD_DIM] / a[HEAD_DIM:HEAD_DIM + 1]
            for g in range(GROUP):
                o_ref[j * tq:(j + 1) * tq, g * HEAD_DIM:(g + 1) * HEAD_DIM] = (
                    o[:, g * tq:(g + 1) * tq].T.astype(BF16))
        if n + 1 < len(stages):
            s, v = s_next, v_next


def _attn_a(qa, kv, kv_ctx, batch, seq, ctx_len, tq=256, tk=512, nblk=4):
    nq = seq // (tq * nblk)
    gw = GROUP * HEAD_DIM
    return pl.pallas_call(
        functools.partial(_attn_a_kernel, tq, tk, nblk),
        out_shape=jax.ShapeDtypeStruct(qa.shape, BF16),
        grid=(batch, A_KV_HEADS, nq),
        in_specs=[
            pl.BlockSpec((tq * nblk, gw), lambda b, h, i: (b * nq + i, h)),
            pl.BlockSpec((ctx_len, HEAD_DIM), lambda b, h, i: (b, h)),
            pl.BlockSpec((ctx_len, HEAD_DIM), lambda b, h, i: (b, A_KV_HEADS + h)),
            pl.BlockSpec((seq, HEAD_DIM), lambda b, h, i: (b, h)),
            pl.BlockSpec((seq, HEAD_DIM), lambda b, h, i: (b, A_KV_HEADS + h)),
        ],
        out_specs=pl.BlockSpec((tq * nblk, gw), lambda b, h, i: (b * nq + i, h)),
        scratch_shapes=[pltpu.VMEM((nblk, HEAD_DIM + BF16_SUBLANES, GROUP * tq), F32)],
        compiler_params=_params(("parallel", "parallel", "arbitrary")),
        name="attn_a",
    )(qa, kv_ctx, kv_ctx, kv, kv)


def _attn_b_kernel(seq, nblk, sink_ref, q_ref, kvc_ref, kv_ref, o_ref):
    span = Q_BLOCK + 2 * WINDOW
    rows = GROUP * Q_BLOCK
    row_q = lax.broadcasted_iota(jnp.int32, (rows, span), 0) % Q_BLOCK
    col_k = lax.broadcasted_iota(jnp.int32, (rows, span), 1)
    work = []
    for j in range(nblk):
        blk = pl.program_id(1) * nblk + j
        qrows = slice(j * Q_BLOCK, (j + 1) * Q_BLOCK)
        start = pl.multiple_of(jnp.clip(blk * Q_BLOCK - WINDOW, 0, seq - span), Q_BLOCK)
        valid = jnp.abs(blk * Q_BLOCK - start + row_q - col_k) <= WINDOW
        for kvh in range(B_KV_HEADS):
            kc = slice(kvh * HEAD_DIM, (kvh + 1) * HEAD_DIM)
            q = _stack_heads(q_ref, kvh, qrows)
            s_win = jnp.where(valid, _dot_nt(q, kv_ref[pl.ds(start, span), kc]), NEG_INF)
            s_ctx = _dot_nt(q, kvc_ref[:, kc])
            work.append((qrows, kvh, start, s_win, s_ctx))
    for qrows, kvh, start, s_win, s_ctx in work:
        vc = slice((B_KV_HEADS + kvh) * HEAD_DIM, (B_KV_HEADS + kvh + 1) * HEAD_DIM)
        o = _softmax_pv([s_win, s_ctx], [kv_ref[pl.ds(start, span), vc], kvc_ref[:, vc]],
                        _sink_column(sink_ref, kvh, Q_BLOCK))
        _unstack_heads(o_ref, kvh, o, Q_BLOCK, qrows)


def _attn_b(qb, kv, kv_ctx, sink, batch, seq, ctx_len, nblk=4):
    nq = seq // (Q_BLOCK * nblk)
    bw = 2 * B_KV_HEADS * HEAD_DIM
    return pl.pallas_call(
        functools.partial(_attn_b_kernel, seq, nblk),
        out_shape=jax.ShapeDtypeStruct(qb.shape, BF16),
        grid_spec=pltpu.PrefetchScalarGridSpec(
            num_scalar_prefetch=1,
            grid=(batch, nq),
            in_specs=[
                pl.BlockSpec((Q_BLOCK * nblk, Q_COLS), lambda b, i, s: (b * nq + i, 0)),
                pl.BlockSpec((ctx_len, bw), lambda b, i, s: (b, 1)),
                pl.BlockSpec((seq, bw), lambda b, i, s: (b, 1)),
            ],
            out_specs=pl.BlockSpec((Q_BLOCK * nblk, Q_COLS), lambda b, i, s: (b * nq + i, 0)),
        ),
        compiler_params=_params(("parallel", "arbitrary")),
        name="attn_b",
    )(sink, qb, kv_ctx, kv)


def _attn_ctx_kernel(ctx_len, sink_ref, qa_ref, qb_ref, kv_ref, oa_ref, ob_ref):
    for kvh in range(A_KV_HEADS):
        k = kv_ref[:, kvh * HEAD_DIM:(kvh + 1) * HEAD_DIM]
        v = kv_ref[:, (A_KV_HEADS + kvh) * HEAD_DIM:(A_KV_HEADS + kvh + 1) * HEAD_DIM]
        o = _softmax_pv([_dot_nt(_stack_heads(qa_ref, kvh), k)], [v])
        _unstack_heads(oa_ref, kvh, o, ctx_len)
    off = 2 * A_KV_HEADS
    for kvh in range(B_KV_HEADS):
        k = kv_ref[:, (off + kvh) * HEAD_DIM:(off + kvh + 1) * HEAD_DIM]
        v = kv_ref[:, (off + B_KV_HEADS + kvh) * HEAD_DIM:(off + B_KV_HEADS + kvh + 1) * HEAD_DIM]
        o = _softmax_pv([_dot_nt(_stack_heads(qb_ref, kvh), k)], [v], _sink_column(sink_ref, kvh, ctx_len))
        _unstack_heads(ob_ref, kvh, o, ctx_len)


def _attn_ctx(qa, qb, kv_ctx, sink, batch, ctx_len):
    q_spec = pl.BlockSpec((ctx_len, Q_COLS), lambda b, s: (b, 0))
    return pl.pallas_call(
        functools.partial(_attn_ctx_kernel, ctx_len),
        out_shape=[jax.ShapeDtypeStruct(qa.shape, BF16), jax.ShapeDtypeStruct(qb.shape, BF16)],
        grid_spec=pltpu.PrefetchScalarGridSpec(
            num_scalar_prefetch=1,
            grid=(batch,),
            in_specs=[q_spec, q_spec, pl.BlockSpec((ctx_len, KV_COLS), lambda b, s: (b, 0))],
            out_specs=[q_spec, q_spec],
        ),
        compiler_params=_params(("parallel",)),
        name="attn_ctx",
    )(sink, qa, qb, kv_ctx)


def _outproj_kernel(tm, blocks_per_seq, oa_ref, ob_ref, z_ref, zprev_ref, znext_ref, gp_ref, cw_ref, w_ref,
                    x_ref, gt_ref, gpost_ref, gffn_ref, sc_ref, sh_ref, xo_ref, h2_ref, oc_scr):
    i = pl.program_id(0)
    pos = i % blocks_per_seq
    z = z_ref[...].astype(F32)
    prev_row = jnp.where(pos == 0, 0.0, zprev_ref[BF16_SUBLANES - 1:BF16_SUBLANES, :].astype(F32))
    next_row = jnp.where(pos == blocks_per_seq - 1, 0.0, znext_ref[0:1, :].astype(F32))
    row = lax.broadcasted_iota(jnp.int32, z.shape, 0)
    z_prev = jnp.where(row == 0, prev_row, pltpu.roll(z, 1, 0))
    z_next = jnp.where(row == tm - 1, next_row, pltpu.roll(z, tm - 1, 0))
    cw = cw_ref[...]
    oc = gp_ref[...].astype(F32) * (z_prev * cw[0:1] + z * cw[1:2] + z_next * cw[2:3])
    oc_scr[...] = oc.astype(BF16)

    post_gain = gt_ref[...] * gpost_ref[...]
    ffn_gain = gffn_ref[...] * (1.0 + sc_ref[...])
    shift = sh_ref[...]
    sub = min(tm, ROW_SUB)
    for r in range(tm // sub):
        rows = slice(r * sub, (r + 1) * sub)
        y = _dot(oa_ref[rows, :], w_ref[0:Q_COLS, :])
        y += _dot(ob_ref[rows, :], w_ref[Q_COLS:2 * Q_COLS, :])
        y += _dot(oc_scr[rows, :], w_ref[2 * Q_COLS:, :])
        x = x_ref[rows, :] + _rms(y) * post_gain
        xo_ref[rows, :] = x
        h2_ref[rows, :] = (_rms(x) * ffn_gain + shift).astype(BF16)


def _outproj(oa, ob, z, gp, conv_w, w_out, layer, x, mod, row_of_block, g_post, g_ffn, tm, seq_len):
    m = x.shape[0]
    blocks_per_seq = seq_len // tm
    halo = tm // BF16_SUBLANES
    last_halo = m // BF16_SUBLANES - 1
    row_spec = lambda w: pl.BlockSpec((tm, w), lambda i: (i, 0))
    return pl.pallas_call(
        functools.partial(_outproj_kernel, tm, blocks_per_seq),
        out_shape=[jax.ShapeDtypeStruct((m, D_MODEL), F32), jax.ShapeDtypeStruct((m, D_MODEL), BF16)],
        grid=(m // tm,),
        in_specs=[
            row_spec(Q_COLS),
            row_spec(Q_COLS),
            row_spec(CONV_CH),
            pl.BlockSpec((BF16_SUBLANES, CONV_CH), lambda i: (jnp.maximum(i * halo - 1, 0), 0)),
            pl.BlockSpec((BF16_SUBLANES, CONV_CH), lambda i: (jnp.minimum((i + 1) * halo, last_halo), 0)),
            row_spec(CONV_CH),
            pl.BlockSpec(conv_w.shape, lambda i: (0, 0)),
            _resident((None, D_MODEL, D_MODEL), lambda i: (layer, 0, 0)),
            row_spec(D_MODEL),
            _mod_spec(2, row_of_block),
            _vec_spec(D_MODEL),
            _vec_spec(D_MODEL),
            _mod_spec(4, row_of_block),
            _mod_spec(3, row_of_block),
        ],
        out_specs=[row_spec(D_MODEL), row_spec(D_MODEL)],
        scratch_shapes=[pltpu.VMEM((tm, CONV_CH), BF16)],
        compiler_params=_params(("parallel",)),
        name="outproj",
    )(oa, ob, z, z, z, gp, conv_w, w_out, x, mod, g_post, g_ffn, mod, mod)


def _ffn_up_kernel(tm, h_ref, wg_ref, wu_ref, a_ref):
    sub = min(tm, ROW_SUB)
    for r in range(tm // sub):
        rows = slice(r * sub, (r + 1) * sub)
        h = h_ref[rows, :]
        a_ref[rows, :] = (_silu(_dot(h, wg_ref[...])) * _dot(h, wu_ref[...])).astype(BF16)


def _ffn_down_kernel(tm, a_ref, wd_ref, x_ref, gt_ref, g_ref, o_ref):
    gain = gt_ref[...] * g_ref[...]
    sub = min(tm, ROW_SUB)
    for r in range(tm // sub):
        rows = slice(r * sub, (r + 1) * sub)
        o_ref[rows, :] = x_ref[rows, :] + _rms(_dot(a_ref[rows, :], wd_ref[...])) * gain


def _ffn(h2, w_gate, w_up, w_down, layer, x, mod, row_of_block, g_post, tm_up, tm_down, tf=512):
    m = x.shape[0]
    hidden = w_gate.shape[-1]
    act = pl.pallas_call(
        functools.partial(_ffn_up_kernel, tm_up),
        out_shape=jax.ShapeDtypeStruct((m, hidden), BF16),
        grid=(m // tm_up, hidden // tf),
        in_specs=[
            pl.BlockSpec((tm_up, D_MODEL), lambda i, f: (i, 0)),
            pl.BlockSpec((None, D_MODEL, tf), lambda i, f: (layer, 0, f)),
            pl.BlockSpec((None, D_MODEL, tf), lambda i, f: (layer, 0, f)),
        ],
        out_specs=pl.BlockSpec((tm_up, tf), lambda i, f: (i, f)),
        compiler_params=_params(("parallel", "arbitrary")),
        name="ffn_up",
    )(h2, w_gate, w_up)
    return pl.pallas_call(
        functools.partial(_ffn_down_kernel, tm_down),
        out_shape=jax.ShapeDtypeStruct((m, D_MODEL), F32),
        grid=(m // tm_down,),
        in_specs=[
            pl.BlockSpec((tm_down, hidden), lambda i: (i, 0)),
            _resident((None, hidden, D_MODEL), lambda i: (layer, 0, 0)),
            pl.BlockSpec((tm_down, D_MODEL), lambda i: (i, 0)),
            _mod_spec(5, row_of_block),
            _vec_spec(D_MODEL),
        ],
        out_specs=pl.BlockSpec((tm_down, D_MODEL), lambda i: (i, 0)),
        compiler_params=_params(("parallel",)),
        name="ffn_down",
    )(act, w_down, x, mod, g_post)


def _rope_tables(seq):
    pos = jnp.arange(seq, dtype=jnp.int32)
    inv_freq = jnp.power(ROPE_BASE, -jnp.arange(ROPE_PAIRS, dtype=F32) / ROPE_PAIRS)
    ang = jnp.stack([pos // GRID_W, pos % GRID_W], axis=1).astype(F32)[:, :, None] * inv_freq
    cos, sin = jnp.cos(ang), jnp.sin(ang)
    cos_t = jnp.concatenate([cos[:, 0], cos[:, 0], cos[:, 1], cos[:, 1]], axis=-1)
    sin_t = jnp.concatenate([-sin[:, 0], sin[:, 0], -sin[:, 1], sin[:, 1]], axis=-1)
    return cos_t, sin_t


def kernel(x, c, ctx, c_ctx, w_ada, b_ada, g_mix_pre, g_mix_post, g_ffn_pre, g_ffn_post, w_in, q_norm, k_norm,
           sink, conv_w, w_out, w_gate, w_up, w_down):
    batch, seq, d = x.shape
    ctx_len = ctx.shape[1]
    depth = w_ada.shape[0]
    assert d == D_MODEL and batch < MOD_ROWS and seq % 512 == 0 and ctx_len % Q_BLOCK == 0

    tm = 512
    tm_ctx = ctx_len
    tm_up = 2048
    tm_ctx_ffn = min(tm, batch * ctx_len)
    lat_row = lambda i: i // (seq // tm)
    ctx_row = lambda i: batch

    cond = jnp.zeros((MOD_ROWS, d), F32).at[:batch].set(c).at[batch].set(c_ctx)
    mod_all = _ada(cond, w_ada, b_ada).reshape(depth, MOD_ROWS * N_MOD, 1, d)
    tables = _rope_tables(seq)
    w_in, w_out = w_in.astype(BF16), w_out.astype(BF16)
    w_gate, w_up, w_down = w_gate.astype(BF16), w_up.astype(BF16), w_down.astype(BF16)

    xl = x.reshape(batch * seq, d)
    xc = ctx.reshape(batch * ctx_len, d)
    vec = lambda a, l: a[l].reshape(1, -1)
    for l in range(depth):
        mod = mod_all[l]
        g_pre, g_post = vec(g_mix_pre, l), vec(g_mix_post, l)
        g_ffn, g_ffn_out = vec(g_ffn_pre, l), vec(g_ffn_post, l)
        qn, kn = vec(q_norm, l), vec(k_norm, l)
        update_ctx = l < depth - 1

        ctx_proj = _inproj(xc, mod, ctx_row, g_pre, w_in, l, qn, kn, None, tm_ctx, ctx_len, kv_only=not update_ctx)
        kv_c = ctx_proj[0]
        kv, qa, qb, z, gp = _inproj(xl, mod, lat_row, g_pre, w_in, l, qn, kn, tables, tm, seq)
        oa = _attn_a(qa, kv, kv_c, batch, seq, ctx_len)
        ob = _attn_b(qb, kv, kv_c, sink[l], batch, seq, ctx_len)
        xl, h2 = _outproj(oa, ob, z, gp, conv_w[l], w_out, l, xl, mod, lat_row, g_post, g_ffn, tm, seq)
        xl = _ffn(h2, w_gate, w_up, w_down, l, xl, mod, lat_row, g_ffn_out, tm_up, tm)
        if update_ctx:
            _, qa_c, qb_c, z_c, gp_c = ctx_proj
            oa_c, ob_c = _attn_ctx(qa_c, qb_c, kv_c, sink[l], batch, ctx_len)
            xc, h2_c = _outproj(oa_c, ob_c, z_c, gp_c, conv_w[l], w_out, l, xc, mod, ctx_row, g_post, g_ffn,
                                tm_ctx, ctx_len)
            xc = _ffn(h2_c, w_gate, w_up, w_down, l, xc, mod, ctx_row, g_ffn_out,
                      min(tm_up, batch * ctx_len), tm_ctx_ffn)
    return xl.reshape(batch, seq, d)
```

```python
import functools

import jax
import jax.numpy as jnp
from jax import lax
from jax.experimental import pallas as pl
from jax.experimental.pallas import tpu as pltpu

F32 = jnp.float32
BF16 = jnp.bfloat16

D_MODEL = 2048
HEAD_DIM = 128
GRID_W = 64
A_Q_HEADS = 6
A_KV_HEADS = 2
B_Q_HEADS = 6
B_KV_HEADS = 2
GROUP = A_Q_HEADS // A_KV_HEADS
Q_COLS = A_Q_HEADS * HEAD_DIM
CONV_CH = D_MODEL - 2 * Q_COLS
WINDOW = 128
Q_BLOCK = 128
ROPE_BASE = 10000.0
ROPE_PAIRS = HEAD_DIM // 4
NORM_EPS = 1e-6
NEG_INF = -1e30
ATTN_SCALE = HEAD_DIM ** -0.5
LOG2_E = 1.4426950408889634
SCORE_SCALE = ATTN_SCALE * LOG2_E
N_MOD = 6
MOD_ROWS = 8

KV_COLS = 4 * A_KV_HEADS * HEAD_DIM
OFF_AQ = KV_COLS
OFF_BQ = OFF_AQ + Q_COLS
OFF_CONV = OFF_BQ + Q_COLS
IN_COLS = OFF_CONV + 3 * CONV_CH

VMEM_LIMIT = 56 * 1024 * 1024
BF16_SUBLANES = 16
ROW_SUB = 256


def _rms(t):
    return t * lax.rsqrt(jnp.mean(t * t, axis=-1, keepdims=True) + NORM_EPS)


def _silu(t):
    return t * (1.0 / (1.0 + jnp.exp(-t)))


def _dot(a, b):
    return jnp.dot(a, b, preferred_element_type=F32)


def _dot_nt(a, b):
    return lax.dot_general(a, b, (((1,), (1,)), ((), ())), preferred_element_type=F32)


def _params(semantics):
    return pltpu.CompilerParams(dimension_semantics=semantics, vmem_limit_bytes=VMEM_LIMIT)


def _resident(shape, index_map):
    return pl.BlockSpec(shape, index_map, pipeline_mode=pl.Buffered(1))


def _ada_kernel(s_ref, w_ref, b_ref, o_ref):
    a = _silu(s_ref[...]).astype(BF16)
    o_ref[0] = _dot(a, w_ref[0].astype(BF16)) + b_ref[0]


def _ada(cond, w_ada, b_ada, tn=1024):
    depth, d, n = w_ada.shape
    return pl.pallas_call(
        _ada_kernel,
        out_shape=jax.ShapeDtypeStruct((depth, MOD_ROWS, n), F32),
        grid=(depth, n // tn),
        in_specs=[
            pl.BlockSpec((MOD_ROWS, d), lambda l, j: (0, 0)),
            pl.BlockSpec((1, d, tn), lambda l, j: (l, 0, j)),
            pl.BlockSpec((1, 1, tn), lambda l, j: (l, 0, j)),
        ],
        out_specs=pl.BlockSpec((1, MOD_ROWS, tn), lambda l, j: (l, 0, j)),
        compiler_params=_params(("parallel", "parallel")),
        name="ada",
    )(cond, w_ada, b_ada.reshape(depth, 1, n))


def _mod_spec(chunk, row_of_block):
    return pl.BlockSpec((None, 1, D_MODEL), lambda i: (row_of_block(i) * N_MOD + chunk, 0, 0))


def _vec_spec(n):
    return pl.BlockSpec((1, n), lambda i: (0, 0))


def _inproj_kernel(rope, kv_only, tm, x_ref, g_ref, sc_ref, sh_ref, w_ref, qn_ref, kn_ref, *rest):
    if rope:
        cos_ref, sin_ref = rest[:2]
        rest = rest[2:]
    if kv_only:
        kv_ref, = rest
    else:
        kv_ref, qa_ref, qb_ref, z_ref, gp_ref = rest
    sub = min(tm, ROW_SUB)
    pre_gain = g_ref[...] * (1.0 + sc_ref[...])
    shift = sh_ref[...]
    k_gain = kn_ref[...] * SCORE_SCALE
    q_gain = qn_ref[...]
    lane = lax.broadcasted_iota(jnp.int32, (sub, HEAD_DIM), 1)
    first_half = (lane & ROPE_PAIRS) == 0

    for r in range(tm // sub):
        rows = slice(r * sub, (r + 1) * sub)
        if rope:
            cos = cos_ref[rows, :]
            sin = sin_ref[rows, :]

            def rot(t):
                partner = jnp.where(first_half, pltpu.roll(t, HEAD_DIM - ROPE_PAIRS, 1),
                                    pltpu.roll(t, ROPE_PAIRS, 1))
                return t * cos + partner * sin
        else:
            def rot(t):
                return t

        hb = (_rms(x_ref[rows, :]) * pre_gain + shift).astype(BF16)

        def proj(c0, n):
            return _dot(hb, w_ref[:, c0:c0 + n])

        def head(t, k):
            return t[:, k * HEAD_DIM:(k + 1) * HEAD_DIM]

        def put(ref, k, t):
            ref[rows, k * HEAD_DIM:(k + 1) * HEAD_DIM] = t.astype(BF16)

        pkv = proj(0, KV_COLS)
        for k in range(A_KV_HEADS):
            put(kv_ref, k, rot(_rms(head(pkv, k)) * k_gain))
            put(kv_ref, A_KV_HEADS + k, head(pkv, A_KV_HEADS + k))
            put(kv_ref, 2 * A_KV_HEADS + k, rot(head(pkv, 2 * A_KV_HEADS + k)) * SCORE_SCALE)
            put(kv_ref, 3 * A_KV_HEADS + k, head(pkv, 3 * A_KV_HEADS + k))
        if kv_only:
            continue

        pq = proj(OFF_AQ, Q_COLS)
        for k in range(A_Q_HEADS):
            put(qa_ref, k, rot(_rms(head(pq, k)) * q_gain))
        pq = proj(OFF_BQ, Q_COLS)
        for k in range(B_Q_HEADS):
            put(qb_ref, k, rot(head(pq, k)))

        z_ref[rows, :] = (proj(OFF_CONV, CONV_CH) * proj(OFF_CONV + 2 * CONV_CH, CONV_CH)).astype(BF16)
        gp_ref[rows, :] = proj(OFF_CONV + CONV_CH, CONV_CH).astype(BF16)


def _inproj(x, mod, row_of_block, g_pre, w_in, layer, q_norm, k_norm, rope_tables, tm, seq_len, kv_only=False):
    m = x.shape[0]
    rope = rope_tables is not None
    cols = KV_COLS if kv_only else IN_COLS
    in_specs = [
        pl.BlockSpec((tm, D_MODEL), lambda i: (i, 0)),
        _vec_spec(D_MODEL),
        _mod_spec(1, row_of_block),
        _mod_spec(0, row_of_block),
        _resident((None, D_MODEL, cols), lambda i: (layer, 0, 0)),
        _vec_spec(HEAD_DIM),
        _vec_spec(HEAD_DIM),
    ]
    args = [x, g_pre, mod, mod, w_in, q_norm, k_norm]
    if rope:
        blocks_per_seq = seq_len // tm
        in_specs += [pl.BlockSpec((tm, HEAD_DIM), lambda i: (i % blocks_per_seq, 0))] * 2
        args += list(rope_tables)
    widths = (KV_COLS,) if kv_only else (KV_COLS, Q_COLS, Q_COLS, CONV_CH, CONV_CH)
    return pl.pallas_call(
        functools.partial(_inproj_kernel, rope, kv_only, tm),
        out_shape=[jax.ShapeDtypeStruct((m, w), BF16) for w in widths],
        grid=(m // tm,),
        in_specs=in_specs,
        out_specs=[pl.BlockSpec((tm, w), lambda i: (i, 0)) for w in widths],
        compiler_params=_params(("parallel",)),
        name=("inproj_rope" if rope else "inproj") + ("_kv" if kv_only else ""),
    )(*args)


def _stack_heads(q_ref, kvh, rows=slice(None)):
    base = kvh * GROUP * HEAD_DIM
    return jnp.concatenate(
        [q_ref[rows, base + g * HEAD_DIM: base + (g + 1) * HEAD_DIM] for g in range(GROUP)], axis=0)


def _unstack_heads(o_ref, kvh, o, tq, rows=slice(None)):
    base = kvh * GROUP * HEAD_DIM
    for g in range(GROUP):
        o_ref[rows, base + g * HEAD_DIM: base + (g + 1) * HEAD_DIM] = o[g * tq:(g + 1) * tq].astype(BF16)


def _sink_column(sink_ref, kvh, tq):
    grp = lax.broadcasted_iota(jnp.int32, (GROUP * tq, 1), 0) // tq
    col = jnp.full((GROUP * tq, 1), sink_ref[kvh * GROUP], F32)
    for g in range(1, GROUP):
        col = jnp.where(grp == g, sink_ref[kvh * GROUP + g], col)
    return col * LOG2_E


def _softmax_pv(scores, values, sink_col=None):
    m = scores[0].max(axis=-1, keepdims=True)
    for s in scores[1:]:
        m = jnp.maximum(m, s.max(axis=-1, keepdims=True))
    if sink_col is not None:
        m = jnp.maximum(m, sink_col)
    den = None
    acc = None
    for s, v in zip(scores, values):
        p = jnp.exp2(s - m)
        d = p.sum(axis=-1, keepdims=True)
        a = _dot(p.astype(BF16), v)
        den = d if den is None else den + d
        acc = a if acc is None else acc + a
    if sink_col is not None:
        den = den + jnp.exp2(sink_col - m)
    return acc / den


def _values_with_ones(v):
    return jnp.concatenate([v.T, jnp.ones((BF16_SUBLANES, v.shape[0]), BF16)], axis=0)


def _attn_a_kernel(tq, tk, nblk, q_ref, kc_ref, vc_ref, k_ref, v_ref, o_ref, acc_scr):
    nc = k_ref.shape[0] // tk

    def scores(stage):
        j, c = stage
        k, v = (kc_ref[...], vc_ref[...]) if c < 0 else (k_ref[c * tk:(c + 1) * tk, :], v_ref[c * tk:(c + 1) * tk, :])
        return _dot_nt(k, qs[j]), v

    stages = [(j, c) for j in range(nblk) for c in range(-1, nc)]
    qs = [_stack_heads(q_ref, 0, slice(j * tq, (j + 1) * tq)) for j in range(nblk)]
    m = None
    s, v = scores(stages[0])
    for n, (j, c) in enumerate(stages):
        if n + 1 < len(stages):
            s_next, v_next = scores(stages[n + 1])
        acc = acc_scr.at[j]
        m_new = s.max(axis=0, keepdims=True)
        if c >= 0:
            m_new = jnp.maximum(m, m_new)
        pv = _dot(_values_with_ones(v), jnp.exp2(s - m_new).astype(BF16))
        if c < 0:
            acc[...] = pv
        else:
            acc[...] = jnp.exp2(m - m_new) * acc[...] + pv
        m = m_new
        if c == nc - 1:
            a = acc[...]
            o = a[:HEAD_DIM] / a[HEAD_DIM:HEAD_DIM + 1]
            for g in range(GROUP):
                o_ref[j * tq:(j + 1) * tq, g * HEAD_DIM:(g + 1) * HEAD_DIM] = (
                    o[:, g * tq:(g + 1) * tq].T.astype(BF16))
        if n + 1 < len(stages):
            s, v = s_next, v_next


def _attn_a(qa, kv, kv_ctx, batch, seq, ctx_len, tq=256, tk=512, nblk=4):
    nq = seq // (tq * nblk)
    gw = GROUP * HEAD_DIM
    return pl.pallas_call(
        functools.partial(_attn_a_kernel, tq, tk, nblk),
        out_shape=jax.ShapeDtypeStruct(qa.shape, BF16),
        grid=(batch, A_KV_HEADS, nq),
        in_specs=[
            pl.BlockSpec((tq * nblk, gw), lambda b, h, i: (b * nq + i, h)),
            pl.BlockSpec((ctx_len, HEAD_DIM), lambda b, h, i: (b, h)),
            pl.BlockSpec((ctx_len, HEAD_DIM), lambda b, h, i: (b, A_KV_HEADS + h)),
            pl.BlockSpec((seq, HEAD_DIM), lambda b, h, i: (b, h)),
            pl.BlockSpec((seq, HEAD_DIM), lambda b, h, i: (b, A_KV_HEADS + h)),
        ],
        out_specs=pl.BlockSpec((tq * nblk, gw), lambda b, h, i: (b * nq + i, h)),
        scratch_shapes=[pltpu.VMEM((nblk, HEAD_DIM + BF16_SUBLANES, GROUP * tq), F32)],
        compiler_params=_params(("parallel", "parallel", "arbitrary")),
        name="attn_a",
    )(qa, kv_ctx, kv_ctx, kv, kv)


def _attn_b_kernel(seq, nblk, sink_ref, q_ref, kvc_ref, kv_ref, o_ref):
    span = Q_BLOCK + 2 * WINDOW
    rows = GROUP * Q_BLOCK
    row_q = lax.broadcasted_iota(jnp.int32, (rows, span), 0) % Q_BLOCK
    col_k = lax.broadcasted_iota(jnp.int32, (rows, span), 1)
    work = []
    for j in range(nblk):
        blk = pl.program_id(1) * nblk + j
        qrows = slice(j * Q_BLOCK, (j + 1) * Q_BLOCK)
        start = pl.multiple_of(jnp.clip(blk * Q_BLOCK - WINDOW, 0, seq - span), Q_BLOCK)
        valid = jnp.abs(blk * Q_BLOCK - start + row_q - col_k) <= WINDOW
        for kvh in range(B_KV_HEADS):
            kc = slice(kvh * HEAD_DIM, (kvh + 1) * HEAD_DIM)
            q = _stack_heads(q_ref, kvh, qrows)
            s_win = jnp.where(valid, _dot_nt(q, kv_ref[pl.ds(start, span), kc]), NEG_INF)
            s_ctx = _dot_nt(q, kvc_ref[:, kc])
            work.append((qrows, kvh, start, s_win, s_ctx))
    for qrows, kvh, start, s_win, s_ctx in work:
        vc = slice((B_KV_HEADS + kvh) * HEAD_DIM, (B_KV_HEADS + kvh + 1) * HEAD_DIM)
        o = _softmax_pv([s_win, s_ctx], [kv_ref[pl.ds(start, span), vc], kvc_ref[:, vc]],
                        _sink_column(sink_ref, kvh, Q_BLOCK))
        _unstack_heads(o_ref, kvh, o, Q_BLOCK, qrows)


def _attn_b(qb, kv, kv_ctx, sink, batch, seq, ctx_len, nblk=4):
    nq = seq // (Q_BLOCK * nblk)
    bw = 2 * B_KV_HEADS * HEAD_DIM
    return pl.pallas_call(
        functools.partial(_attn_b_kernel, seq, nblk),
        out_shape=jax.ShapeDtypeStruct(qb.shape, BF16),
        grid_spec=pltpu.PrefetchScalarGridSpec(
            num_scalar_prefetch=1,
            grid=(batch, nq),
            in_specs=[
                pl.BlockSpec((Q_BLOCK * nblk, Q_COLS), lambda b, i, s: (b * nq + i, 0)),
                pl.BlockSpec((ctx_len, bw), lambda b, i, s: (b, 1)),
                pl.BlockSpec((seq, bw), lambda b, i, s: (b, 1)),
            ],
            out_specs=pl.BlockSpec((Q_BLOCK * nblk, Q_COLS), lambda b, i, s: (b * nq + i, 0)),
        ),
        compiler_params=_params(("parallel", "arbitrary")),
        name="attn_b",
    )(sink, qb, kv_ctx, kv)


def _attn_ctx_kernel(ctx_len, sink_ref, qa_ref, qb_ref, kv_ref, oa_ref, ob_ref):
    for kvh in range(A_KV_HEADS):
        k = kv_ref[:, kvh * HEAD_DIM:(kvh + 1) * HEAD_DIM]
        v = kv_ref[:, (A_KV_HEADS + kvh) * HEAD_DIM:(A_KV_HEADS + kvh + 1) * HEAD_DIM]
        o = _softmax_pv([_dot_nt(_stack_heads(qa_ref, kvh), k)], [v])
        _unstack_heads(oa_ref, kvh, o, ctx_len)
    off = 2 * A_KV_HEADS
    for kvh in range(B_KV_HEADS):
        k = kv_ref[:, (off + kvh) * HEAD_DIM:(off + kvh + 1) * HEAD_DIM]
        v = kv_ref[:, (off + B_KV_HEADS + kvh) * HEAD_DIM:(off + B_KV_HEADS + kvh + 1) * HEAD_DIM]
        o = _softmax_pv([_dot_nt(_stack_heads(qb_ref, kvh), k)], [v], _sink_column(sink_ref, kvh, ctx_len))
        _unstack_heads(ob_ref, kvh, o, ctx_len)


def _attn_ctx(qa, qb, kv_ctx, sink, batch, ctx_len):
    q_spec = pl.BlockSpec((ctx_len, Q_COLS), lambda b, s: (b, 0))
    return pl.pallas_call(
        functools.partial(_attn_ctx_kernel, ctx_len),
        out_shape=[jax.ShapeDtypeStruct(qa.shape, BF16), jax.ShapeDtypeStruct(qb.shape, BF16)],
        grid_spec=pltpu.PrefetchScalarGridSpec(
            num_scalar_prefetch=1,
            grid=(batch,),
            in_specs=[q_spec, q_spec, pl.BlockSpec((ctx_len, KV_COLS), lambda b, s: (b, 0))],
            out_specs=[q_spec, q_spec],
        ),
        compiler_params=_params(("parallel",)),
        name="attn_ctx",
    )(sink, qa, qb, kv_ctx)


def _outproj_kernel(tm, blocks_per_seq, oa_ref, ob_ref, z_ref, zprev_ref, znext_ref, gp_ref, cw_ref, w_ref,
                    x_ref, gt_ref, gpost_ref, gffn_ref, sc_ref, sh_ref, xo_ref, h2_ref, oc_scr):
    i = pl.program_id(0)
    pos = i % blocks_per_seq
    z = z_ref[...].astype(F32)
    prev_row = jnp.where(pos == 0, 0.0, zprev_ref[BF16_SUBLANES - 1:BF16_SUBLANES, :].astype(F32))
    next_row = jnp.where(pos == blocks_per_seq - 1, 0.0, znext_ref[0:1, :].astype(F32))
    row = lax.broadcasted_iota(jnp.int32, z.shape, 0)
    z_prev = jnp.where(row == 0, prev_row, pltpu.roll(z, 1, 0))
    z_next = jnp.where(row == tm - 1, next_row, pltpu.roll(z, tm - 1, 0))
    cw = cw_ref[...]
    oc = gp_ref[...].astype(F32) * (z_prev * cw[0:1] + z * cw[1:2] + z_next * cw[2:3])
    oc_scr[...] = oc.astype(BF16)

    post_gain = gt_ref[...] * gpost_ref[...]
    ffn_gain = gffn_ref[...] * (1.0 + sc_ref[...])
    shift = sh_ref[...]
    sub = min(tm, ROW_SUB)
    for r in range(tm // sub):
        rows = slice(r * sub, (r + 1) * sub)
        y = _dot(oa_ref[rows, :], w_ref[0:Q_COLS, :])
        y += _dot(ob_ref[rows, :], w_ref[Q_COLS:2 * Q_COLS, :])
        y += _dot(oc_scr[rows, :], w_ref[2 * Q_COLS:, :])
        x = x_ref[rows, :] + _rms(y) * post_gain
        xo_ref[rows, :] = x
        h2_ref[rows, :] = (_rms(x) * ffn_gain + shift).astype(BF16)


def _outproj(oa, ob, z, gp, conv_w, w_out, layer, x, mod, row_of_block, g_post, g_ffn, tm, seq_len):
    m = x.shape[0]
    blocks_per_seq = seq_len // tm
    halo = tm // BF16_SUBLANES
    last_halo = m // BF16_SUBLANES - 1
    row_spec = lambda w: pl.BlockSpec((tm, w), lambda i: (i, 0))
    return pl.pallas_call(
        functools.partial(_outproj_kernel, tm, blocks_per_seq),
        out_shape=[jax.ShapeDtypeStruct((m, D_MODEL), F32), jax.ShapeDtypeStruct((m, D_MODEL), BF16)],
        grid=(m // tm,),
        in_specs=[
            row_spec(Q_COLS),
            row_spec(Q_COLS),
            row_spec(CONV_CH),
            pl.BlockSpec((BF16_SUBLANES, CONV_CH), lambda i: (jnp.maximum(i * halo - 1, 0), 0)),
            pl.BlockSpec((BF16_SUBLANES, CONV_CH), lambda i: (jnp.minimum((i + 1) * halo, last_halo), 0)),
            row_spec(CONV_CH),
            pl.BlockSpec(conv_w.shape, lambda i: (0, 0)),
            _resident((None, D_MODEL, D_MODEL), lambda i: (layer, 0, 0)),
            row_spec(D_MODEL),
            _mod_spec(2, row_of_block),
            _vec_spec(D_MODEL),
            _vec_spec(D_MODEL),
            _mod_spec(4, row_of_block),
            _mod_spec(3, row_of_block),
        ],
        out_specs=[row_spec(D_MODEL), row_spec(D_MODEL)],
        scratch_shapes=[pltpu.VMEM((tm, CONV_CH), BF16)],
        compiler_params=_params(("parallel",)),
        name="outproj",
    )(oa, ob, z, z, z, gp, conv_w, w_out, x, mod, g_post, g_ffn, mod, mod)


def _ffn_up_kernel(tm, round_down, h_ref, wg_ref, wu_ref, *rest):
    if round_down:
        wd_ref, a_ref, wd16_ref = rest
        wd16_ref[...] = wd_ref[...].astype(BF16)
    else:
        a_ref, = rest
    wg = wg_ref[...].astype(BF16)
    wu = wu_ref[...].astype(BF16)
    sub = min(tm, ROW_SUB)
    for r in range(tm // sub):
        rows = slice(r * sub, (r + 1) * sub)
        h = h_ref[rows, :]
        a_ref[rows, :] = (_silu(_dot(h, wg)) * _dot(h, wu)).astype(BF16)


def _ffn_down_kernel(tm, a_ref, wd_ref, x_ref, gt_ref, g_ref, o_ref):
    gain = gt_ref[...] * g_ref[...]
    sub = min(tm, ROW_SUB)
    for r in range(tm // sub):
        rows = slice(r * sub, (r + 1) * sub)
        o_ref[rows, :] = x_ref[rows, :] + _rms(_dot(a_ref[rows, :], wd_ref[...])) * gain


def _ffn(h2, w_gate, w_up, w_down, layer, x, mod, row_of_block, g_post, tm_up, tm_down, w_down16=None, tf=512):
    m = x.shape[0]
    hidden = w_gate.shape[-1]
    nf = hidden // tf
    round_down = w_down16 is None
    in_specs = [
        pl.BlockSpec((tm_up, D_MODEL), lambda i, f: (i, 0)),
        pl.BlockSpec((None, D_MODEL, tf), lambda i, f: (layer, 0, f)),
        pl.BlockSpec((None, D_MODEL, tf), lambda i, f: (layer, 0, f)),
    ]
    out_shape = [jax.ShapeDtypeStruct((m, hidden), BF16)]
    out_specs = [pl.BlockSpec((tm_up, tf), lambda i, f: (i, f))]
    args = [h2, w_gate, w_up]
    if round_down:
        slab = hidden // ((m // tm_up) * nf)
        assert slab % BF16_SUBLANES == 0 and slab * (m // tm_up) * nf == hidden
        in_specs.append(pl.BlockSpec((None, slab, D_MODEL), lambda i, f: (layer, i * nf + f, 0)))
        out_shape.append(jax.ShapeDtypeStruct((hidden, D_MODEL), BF16))
        out_specs.append(pl.BlockSpec((slab, D_MODEL), lambda i, f: (i * nf + f, 0)))
        args.append(w_down)
    outs = pl.pallas_call(
        functools.partial(_ffn_up_kernel, tm_up, round_down),
        out_shape=out_shape,
        grid=(m // tm_up, nf),
        in_specs=in_specs,
        out_specs=out_specs,
        compiler_params=_params(("parallel", "arbitrary")),
        name="ffn_up",
    )(*args)
    act = outs[0]
    if round_down:
        w_down16 = outs[1]
    out = pl.pallas_call(
        functools.partial(_ffn_down_kernel, tm_down),
        out_shape=jax.ShapeDtypeStruct((m, D_MODEL), F32),
        grid=(m // tm_down,),
        in_specs=[
            pl.BlockSpec((tm_down, hidden), lambda i: (i, 0)),
            _resident((hidden, D_MODEL), lambda i: (0, 0)),
            pl.BlockSpec((tm_down, D_MODEL), lambda i: (i, 0)),
            _mod_spec(5, row_of_block),
            _vec_spec(D_MODEL),
        ],
        out_specs=pl.BlockSpec((tm_down, D_MODEL), lambda i: (i, 0)),
        compiler_params=_params(("parallel",)),
        name="ffn_down",
    )(act, w_down16, x, mod, g_post)
    return out, w_down16


def _rope_tables(seq):
    pos = jnp.arange(seq, dtype=jnp.int32)
    inv_freq = jnp.power(ROPE_BASE, -jnp.arange(ROPE_PAIRS, dtype=F32) / ROPE_PAIRS)
    ang = jnp.stack([pos // GRID_W, pos % GRID_W], axis=1).astype(F32)[:, :, None] * inv_freq
    cos, sin = jnp.cos(ang), jnp.sin(ang)
    cos_t = jnp.concatenate([cos[:, 0], cos[:, 0], cos[:, 1], cos[:, 1]], axis=-1)
    sin_t = jnp.concatenate([-sin[:, 0], sin[:, 0], -sin[:, 1], sin[:, 1]], axis=-1)
    return cos_t, sin_t


def kernel(x, c, ctx, c_ctx, w_ada, b_ada, g_mix_pre, g_mix_post, g_ffn_pre, g_ffn_post, w_in, q_norm, k_norm,
           sink, conv_w, w_out, w_gate, w_up, w_down):
    batch, seq, d = x.shape
    ctx_len = ctx.shape[1]
    depth = w_ada.shape[0]
    assert d == D_MODEL and batch < MOD_ROWS and seq % 512 == 0 and ctx_len % Q_BLOCK == 0

    tm = 512
    tm_ctx = ctx_len
    tm_up = 2048
    tm_ctx_ffn = min(tm, batch * ctx_len)
    lat_row = lambda i: i // (seq // tm)
    ctx_row = lambda i: batch

    cond = jnp.zeros((MOD_ROWS, d), F32).at[:batch].set(c).at[batch].set(c_ctx)
    mod_all = _ada(cond, w_ada, b_ada).reshape(depth, MOD_ROWS * N_MOD, 1, d)
    tables = _rope_tables(seq)
    w_in, w_out = w_in.astype(BF16), w_out.astype(BF16)

    xl = x.reshape(batch * seq, d)
    xc = ctx.reshape(batch * ctx_len, d)
    vec = lambda a, l: a[l].reshape(1, -1)
    for l in range(depth):
        mod = mod_all[l]
        g_pre, g_post = vec(g_mix_pre, l), vec(g_mix_post, l)
        g_ffn, g_ffn_out = vec(g_ffn_pre, l), vec(g_ffn_post, l)
        qn, kn = vec(q_norm, l), vec(k_norm, l)
        update_ctx = l < depth - 1

        ctx_proj = _inproj(xc, mod, ctx_row, g_pre, w_in, l, qn, kn, None, tm_ctx, ctx_len, kv_only=not update_ctx)
        kv_c = ctx_proj[0]
        kv, qa, qb, z, gp = _inproj(xl, mod, lat_row, g_pre, w_in, l, qn, kn, tables, tm, seq)
        oa = _attn_a(qa, kv, kv_c, batch, seq, ctx_len)
        ob = _attn_b(qb, kv, kv_c, sink[l], batch, seq, ctx_len)
        xl, h2 = _outproj(oa, ob, z, gp, conv_w[l], w_out, l, xl, mod, lat_row, g_post, g_ffn, tm, seq)
        xl, w_down16 = _ffn(h2, w_gate, w_up, w_down, l, xl, mod, lat_row, g_ffn_out, tm_up, tm)
        if update_ctx:
            _, qa_c, qb_c, z_c, gp_c = ctx_proj
            oa_c, ob_c = _attn_ctx(qa_c, qb_c, kv_c, sink[l], batch, ctx_len)
            xc, h2_c = _outproj(oa_c, ob_c, z_c, gp_c, conv_w[l], w_out, l, xc, mod, ctx_row, g_post, g_ffn,
                                tm_ctx, ctx_len)
            xc, _ = _ffn(h2_c, w_gate, w_up, w_down, l, xc, mod, ctx_row, g_ffn_out,
                         min(tm_up, batch * ctx_len), tm_ctx_ffn, w_down16)
    return xl.reshape(batch, seq, d)
```

```python
import functools

import jax
import jax.numpy as jnp
from jax import lax
from jax.experimental import pallas as pl
from jax.experimental.pallas import tpu as pltpu

F32 = jnp.float32
BF16 = jnp.bfloat16

D_MODEL = 2048
HEAD_DIM = 128
GRID_W = 64
A_Q_HEADS = 6
A_KV_HEADS = 2
B_Q_HEADS = 6
B_KV_HEADS = 2
GROUP = A_Q_HEADS // A_KV_HEADS
Q_COLS = A_Q_HEADS * HEAD_DIM
CONV_CH = D_MODEL - 2 * Q_COLS
WINDOW = 128
Q_BLOCK = 128
ROPE_BASE = 10000.0
ROPE_PAIRS = HEAD_DIM // 4
NORM_EPS = 1e-6
NEG_INF = -1e30
ATTN_SCALE = HEAD_DIM ** -0.5
LOG2_E = 1.4426950408889634
SCORE_SCALE = ATTN_SCALE * LOG2_E
N_MOD = 6
MOD_ROWS = 8

KV_COLS = 4 * A_KV_HEADS * HEAD_DIM
OFF_AQ = KV_COLS
OFF_BQ = OFF_AQ + Q_COLS
OFF_CONV = OFF_BQ + Q_COLS
IN_COLS = OFF_CONV + 3 * CONV_CH

VMEM_LIMIT = 56 * 1024 * 1024
BF16_SUBLANES = 16
ROW_SUB = 256


def _rms(t):
    return t * lax.rsqrt(jnp.mean(t * t, axis=-1, keepdims=True) + NORM_EPS)


def _silu(t):
    return t * (1.0 / (1.0 + jnp.exp(-t)))


def _dot(a, b):
    return jnp.dot(a, b, preferred_element_type=F32)


def _dot_nt(a, b):
    return lax.dot_general(a, b, (((1,), (1,)), ((), ())), preferred_element_type=F32)


def _params(semantics):
    return pltpu.CompilerParams(dimension_semantics=semantics, vmem_limit_bytes=VMEM_LIMIT)


def _resident(shape, index_map):
    return pl.BlockSpec(shape, index_map, pipeline_mode=pl.Buffered(1))


def _ada_kernel(s_ref, w_ref, b_ref, o_ref):
    a = _silu(s_ref[...]).astype(BF16)
    o_ref[0] = _dot(a, w_ref[0].astype(BF16)) + b_ref[0]


def _ada(cond, w_ada, b_ada, tn=1024):
    depth, d, n = w_ada.shape
    return pl.pallas_call(
        _ada_kernel,
        out_shape=jax.ShapeDtypeStruct((depth, MOD_ROWS, n), F32),
        grid=(depth, n // tn),
        in_specs=[
            pl.BlockSpec((MOD_ROWS, d), lambda l, j: (0, 0)),
            pl.BlockSpec((1, d, tn), lambda l, j: (l, 0, j)),
            pl.BlockSpec((1, 1, tn), lambda l, j: (l, 0, j)),
        ],
        out_specs=pl.BlockSpec((1, MOD_ROWS, tn), lambda l, j: (l, 0, j)),
        compiler_params=_params(("parallel", "parallel")),
        name="ada",
    )(cond, w_ada, b_ada.reshape(depth, 1, n))


def _mod_spec(chunk, row_of_block):
    return pl.BlockSpec((None, 1, D_MODEL), lambda i: (row_of_block(i) * N_MOD + chunk, 0, 0))


def _vec_spec(n):
    return pl.BlockSpec((1, n), lambda i: (0, 0))


def _inproj_kernel(rope, kv_only, tm, x_ref, g_ref, sc_ref, sh_ref, w_ref, qn_ref, kn_ref, *rest):
    if rope:
        cos_ref, sin_ref = rest[:2]
        rest = rest[2:]
    if kv_only:
        kv_ref, = rest
    else:
        kv_ref, qa_ref, qb_ref, z_ref, gp_ref = rest
    sub = min(tm, ROW_SUB)
    pre_gain = g_ref[...] * (1.0 + sc_ref[...])
    shift = sh_ref[...]
    k_gain = kn_ref[...] * SCORE_SCALE
    q_gain = qn_ref[...]
    lane = lax.broadcasted_iota(jnp.int32, (sub, HEAD_DIM), 1)
    first_half = (lane & ROPE_PAIRS) == 0

    for r in range(tm // sub):
        rows = slice(r * sub, (r + 1) * sub)
        if rope:
            cos = cos_ref[rows, :]
            sin = sin_ref[rows, :]

            def rot(t):
                partner = jnp.where(first_half, pltpu.roll(t, HEAD_DIM - ROPE_PAIRS, 1),
                                    pltpu.roll(t, ROPE_PAIRS, 1))
                return t * cos + partner * sin
        else:
            def rot(t):
                return t

        hb = (_rms(x_ref[rows, :]) * pre_gain + shift).astype(BF16)

        def proj(c0, n):
            return _dot(hb, w_ref[:, c0:c0 + n])

        def head(t, k):
            return t[:, k * HEAD_DIM:(k + 1) * HEAD_DIM]

        def put(ref, k, t):
            ref[rows, k * HEAD_DIM:(k + 1) * HEAD_DIM] = t.astype(BF16)

        pkv = proj(0, KV_COLS)
        for k in range(A_KV_HEADS):
            put(kv_ref, k, rot(_rms(head(pkv, k)) * k_gain))
            put(kv_ref, A_KV_HEADS + k, head(pkv, A_KV_HEADS + k))
            put(kv_ref, 2 * A_KV_HEADS + k, rot(head(pkv, 2 * A_KV_HEADS + k)) * SCORE_SCALE)
            put(kv_ref, 3 * A_KV_HEADS + k, head(pkv, 3 * A_KV_HEADS + k))
        if kv_only:
            continue

        pq = proj(OFF_AQ, Q_COLS)
        for k in range(A_Q_HEADS):
            put(qa_ref, k, rot(_rms(head(pq, k)) * q_gain))
        pq = proj(OFF_BQ, Q_COLS)
        for k in range(B_Q_HEADS):
            put(qb_ref, k, rot(head(pq, k)))

        z_ref[rows, :] = (proj(OFF_CONV, CONV_CH) * proj(OFF_CONV + 2 * CONV_CH, CONV_CH)).astype(BF16)
        gp_ref[rows, :] = proj(OFF_CONV + CONV_CH, CONV_CH).astype(BF16)


def _inproj(x, mod, row_of_block, g_pre, w_in, q_norm, k_norm, rope_tables, tm, seq_len, kv_only=False):
    m = x.shape[0]
    rope = rope_tables is not None
    cols = KV_COLS if kv_only else IN_COLS
    in_specs = [
        pl.BlockSpec((tm, D_MODEL), lambda i: (i, 0)),
        _vec_spec(D_MODEL),
        _mod_spec(1, row_of_block),
        _mod_spec(0, row_of_block),
        _resident((D_MODEL, cols), lambda i: (0, 0)),
        _vec_spec(HEAD_DIM),
        _vec_spec(HEAD_DIM),
    ]
    args = [x, g_pre, mod, mod, w_in, q_norm, k_norm]
    if rope:
        blocks_per_seq = seq_len // tm
        in_specs += [pl.BlockSpec((tm, HEAD_DIM), lambda i: (i % blocks_per_seq, 0))] * 2
        args += list(rope_tables)
    widths = (KV_COLS,) if kv_only else (KV_COLS, Q_COLS, Q_COLS, CONV_CH, CONV_CH)
    return pl.pallas_call(
        functools.partial(_inproj_kernel, rope, kv_only, tm),
        out_shape=[jax.ShapeDtypeStruct((m, w), BF16) for w in widths],
        grid=(m // tm,),
        in_specs=in_specs,
        out_specs=[pl.BlockSpec((tm, w), lambda i: (i, 0)) for w in widths],
        compiler_params=_params(("parallel",)),
        name=("inproj_rope" if rope else "inproj") + ("_kv" if kv_only else ""),
    )(*args)


def _stack_heads(q_ref, kvh, rows=slice(None)):
    base = kvh * GROUP * HEAD_DIM
    return jnp.concatenate(
        [q_ref[rows, base + g * HEAD_DIM: base + (g + 1) * HEAD_DIM] for g in range(GROUP)], axis=0)


def _unstack_heads(o_ref, kvh, o, tq, rows=slice(None)):
    base = kvh * GROUP * HEAD_DIM
    for g in range(GROUP):
        o_ref[rows, base + g * HEAD_DIM: base + (g + 1) * HEAD_DIM] = o[g * tq:(g + 1) * tq].astype(BF16)


def _sink_column(sink_ref, kvh, tq):
    grp = lax.broadcasted_iota(jnp.int32, (GROUP * tq, 1), 0) // tq
    col = jnp.full((GROUP * tq, 1), sink_ref[kvh * GROUP], F32)
    for g in range(1, GROUP):
        col = jnp.where(grp == g, sink_ref[kvh * GROUP + g], col)
    return col * LOG2_E


def _softmax_pv(scores, values, sink_col=None):
    tiles = [s[:, t * HEAD_DIM:(t + 1) * HEAD_DIM] for s in scores for t in range(s.shape[1] // HEAD_DIM)]
    m = functools.reduce(jnp.maximum, tiles).max(axis=-1, keepdims=True)
    if sink_col is not None:
        m = jnp.maximum(m, sink_col)
    acc = None
    for s, v in zip(scores, values):
        v1 = jnp.concatenate([v, jnp.ones_like(v)], axis=1)
        a = _dot(jnp.exp2(s - m).astype(BF16), v1)
        acc = a if acc is None else acc + a
    den = acc[:, HEAD_DIM:]
    if sink_col is not None:
        den = den + jnp.exp2(sink_col - m)
    return acc[:, :HEAD_DIM] / den


def _values_with_ones(v):
    return jnp.concatenate([v.T, jnp.ones((BF16_SUBLANES, v.shape[0]), BF16)], axis=0)


def _attn_a_kernel(tq, tk, nblk, q_ref, kc_ref, vc_ref, k_ref, v_ref, o_ref, acc_scr):
    nc = k_ref.shape[0] // tk

    def scores(stage):
        j, c = stage
        k, v = (kc_ref[...], vc_ref[...]) if c < 0 else (k_ref[c * tk:(c + 1) * tk, :], v_ref[c * tk:(c + 1) * tk, :])
        return _dot_nt(k, qs[j]), v

    stages = [(j, c) for j in range(nblk) for c in range(-1, nc)]
    qs = [_stack_heads(q_ref, 0, slice(j * tq, (j + 1) * tq)) for j in range(nblk)]
    m = None
    s, v = scores(stages[0])
    for n, (j, c) in enumerate(stages):
        if n + 1 < len(stages):
            s_next, v_next = scores(stages[n + 1])
        acc = acc_scr.at[j]
        m_new = s.max(axis=0, keepdims=True)
        if c >= 0:
            m_new = jnp.maximum(m, m_new)
        pv = _dot(_values_with_ones(v), jnp.exp2(s - m_new).astype(BF16))
        if c < 0:
            acc[...] = pv
        else:
            acc[...] = jnp.exp2(m - m_new) * acc[...] + pv
        m = m_new
        if c == nc - 1:
            a = acc[...]
            o = a[:HEAD_DIM] / a[HEAD_DIM:HEAD_DIM + 1]
            for g in range(GROUP):
                o_ref[j * tq:(j + 1) * tq, g * HEAD_DIM:(g + 1) * HEAD_DIM] = (
                    o[:, g * tq:(g + 1) * tq].T.astype(BF16))
        if n + 1 < len(stages):
            s, v = s_next, v_next


def _attn_a(qa, kv, kv_ctx, batch, seq, ctx_len, tq=256, tk=512, nblk=4):
    nq = seq // (tq * nblk)
    gw = GROUP * HEAD_DIM
    return pl.pallas_call(
        functools.partial(_attn_a_kernel, tq, tk, nblk),
        out_shape=jax.ShapeDtypeStruct(qa.shape, BF16),
        grid=(batch, A_KV_HEADS, nq),
        in_specs=[
            pl.BlockSpec((tq * nblk, gw), lambda b, h, i: (b * nq + i, h)),
            pl.BlockSpec((ctx_len, HEAD_DIM), lambda b, h, i: (b, h)),
            pl.BlockSpec((ctx_len, HEAD_DIM), lambda b, h, i: (b, A_KV_HEADS + h)),
            pl.BlockSpec((seq, HEAD_DIM), lambda b, h, i: (b, h)),
            pl.BlockSpec((seq, HEAD_DIM), lambda b, h, i: (b, A_KV_HEADS + h)),
        ],
        out_specs=pl.BlockSpec((tq * nblk, gw), lambda b, h, i: (b * nq + i, h)),
        scratch_shapes=[pltpu.VMEM((nblk, HEAD_DIM + BF16_SUBLANES, GROUP * tq), F32)],
        compiler_params=_params(("parallel", "parallel", "arbitrary")),
        name="attn_a",
    )(qa, kv_ctx, kv_ctx, kv, kv)


def _attn_b_kernel(seq, nblk, sink_ref, q_ref, kvc_ref, kv_ref, o_ref):
    span = Q_BLOCK + 2 * WINDOW
    rows = GROUP * Q_BLOCK
    row_q = lax.broadcasted_iota(jnp.int32, (rows, span), 0) % Q_BLOCK
    col_k = lax.broadcasted_iota(jnp.int32, (rows, span), 1)
    work = []
    for j in range(nblk):
        blk = pl.program_id(1) * nblk + j
        qrows = slice(j * Q_BLOCK, (j + 1) * Q_BLOCK)
        start = pl.multiple_of(jnp.clip(blk * Q_BLOCK - WINDOW, 0, seq - span), Q_BLOCK)
        valid = jnp.abs(blk * Q_BLOCK - start + row_q - col_k) <= WINDOW
        for kvh in range(B_KV_HEADS):
            kc = slice(kvh * HEAD_DIM, (kvh + 1) * HEAD_DIM)
            q = _stack_heads(q_ref, kvh, qrows)
            s_win = jnp.where(valid, _dot_nt(q, kv_ref[pl.ds(start, span), kc]), NEG_INF)
            s_ctx = _dot_nt(q, kvc_ref[:, kc])
            work.append((qrows, kvh, start, s_win, s_ctx))
    for qrows, kvh, start, s_win, s_ctx in work:
        vc = slice((B_KV_HEADS + kvh) * HEAD_DIM, (B_KV_HEADS + kvh + 1) * HEAD_DIM)
        o = _softmax_pv([s_win, s_ctx], [kv_ref[pl.ds(start, span), vc], kvc_ref[:, vc]],
                        _sink_column(sink_ref, kvh, Q_BLOCK))
        _unstack_heads(o_ref, kvh, o, Q_BLOCK, qrows)


def _attn_b(qb, kv, kv_ctx, sink, batch, seq, ctx_len, nblk=4):
    nq = seq // (Q_BLOCK * nblk)
    bw = 2 * B_KV_HEADS * HEAD_DIM
    return pl.pallas_call(
        functools.partial(_attn_b_kernel, seq, nblk),
        out_shape=jax.ShapeDtypeStruct(qb.shape, BF16),
        grid_spec=pltpu.PrefetchScalarGridSpec(
            num_scalar_prefetch=1,
            grid=(batch, nq),
            in_specs=[
                pl.BlockSpec((Q_BLOCK * nblk, Q_COLS), lambda b, i, s: (b * nq + i, 0)),
                pl.BlockSpec((ctx_len, bw), lambda b, i, s: (b, 1)),
                pl.BlockSpec((seq, bw), lambda b, i, s: (b, 1)),
            ],
            out_specs=pl.BlockSpec((Q_BLOCK * nblk, Q_COLS), lambda b, i, s: (b * nq + i, 0)),
        ),
        compiler_params=_params(("parallel", "arbitrary")),
        name="attn_b",
    )(sink, qb, kv_ctx, kv)


def _attn_ctx_kernel(ctx_len, sink_ref, qa_ref, qb_ref, kv_ref, oa_ref, ob_ref):
    for kvh in range(A_KV_HEADS):
        k = kv_ref[:, kvh * HEAD_DIM:(kvh + 1) * HEAD_DIM]
        v = kv_ref[:, (A_KV_HEADS + kvh) * HEAD_DIM:(A_KV_HEADS + kvh + 1) * HEAD_DIM]
        o = _softmax_pv([_dot_nt(_stack_heads(qa_ref, kvh), k)], [v])
        _unstack_heads(oa_ref, kvh, o, ctx_len)
    off = 2 * A_KV_HEADS
    for kvh in range(B_KV_HEADS):
        k = kv_ref[:, (off + kvh) * HEAD_DIM:(off + kvh + 1) * HEAD_DIM]
        v = kv_ref[:, (off + B_KV_HEADS + kvh) * HEAD_DIM:(off + B_KV_HEADS + kvh + 1) * HEAD_DIM]
        o = _softmax_pv([_dot_nt(_stack_heads(qb_ref, kvh), k)], [v], _sink_column(sink_ref, kvh, ctx_len))
        _unstack_heads(ob_ref, kvh, o, ctx_len)


def _attn_ctx(qa, qb, kv_ctx, sink, batch, ctx_len):
    q_spec = pl.BlockSpec((ctx_len, Q_COLS), lambda b, s: (b, 0))
    return pl.pallas_call(
        functools.partial(_attn_ctx_kernel, ctx_len),
        out_shape=[jax.ShapeDtypeStruct(qa.shape, BF16), jax.ShapeDtypeStruct(qb.shape, BF16)],
        grid_spec=pltpu.PrefetchScalarGridSpec(
            num_scalar_prefetch=1,
            grid=(batch,),
            in_specs=[q_spec, q_spec, pl.BlockSpec((ctx_len, KV_COLS), lambda b, s: (b, 0))],
            out_specs=[q_spec, q_spec],
        ),
        compiler_params=_params(("parallel",)),
        name="attn_ctx",
    )(sink, qa, qb, kv_ctx)


def _outproj_kernel(tm, blocks_per_seq, oa_ref, ob_ref, z_ref, zprev_ref, znext_ref, gp_ref, cw_ref, w_ref,
                    x_ref, gt_ref, gpost_ref, gffn_ref, sc_ref, sh_ref, xo_ref, h2_ref, oc_scr):
    i = pl.program_id(0)
    pos = i % blocks_per_seq
    z = z_ref[...].astype(F32)
    prev_row = jnp.where(pos == 0, 0.0, zprev_ref[BF16_SUBLANES - 1:BF16_SUBLANES, :].astype(F32))
    next_row = jnp.where(pos == blocks_per_seq - 1, 0.0, znext_ref[0:1, :].astype(F32))
    row = lax.broadcasted_iota(jnp.int32, z.shape, 0)
    z_prev = jnp.where(row == 0, prev_row, pltpu.roll(z, 1, 0))
    z_next = jnp.where(row == tm - 1, next_row, pltpu.roll(z, tm - 1, 0))
    cw = cw_ref[...]
    oc = gp_ref[...].astype(F32) * (z_prev * cw[0:1] + z * cw[1:2] + z_next * cw[2:3])
    oc_scr[...] = oc.astype(BF16)

    post_gain = gt_ref[...] * gpost_ref[...]
    ffn_gain = gffn_ref[...] * (1.0 + sc_ref[...])
    shift = sh_ref[...]
    sub = min(tm, ROW_SUB)
    for r in range(tm // sub):
        rows = slice(r * sub, (r + 1) * sub)
        y = _dot(oa_ref[rows, :], w_ref[0:Q_COLS, :])
        y += _dot(ob_ref[rows, :], w_ref[Q_COLS:2 * Q_COLS, :])
        y += _dot(oc_scr[rows, :], w_ref[2 * Q_COLS:, :])
        x = x_ref[rows, :] + _rms(y) * post_gain
        xo_ref[rows, :] = x
        h2_ref[rows, :] = (_rms(x) * ffn_gain + shift).astype(BF16)


def _outproj(oa, ob, z, gp, conv_w, w_out, x, mod, row_of_block, g_post, g_ffn, tm, seq_len):
    m = x.shape[0]
    blocks_per_seq = seq_len // tm
    halo = tm // BF16_SUBLANES
    last_halo = m // BF16_SUBLANES - 1
    row_spec = lambda w: pl.BlockSpec((tm, w), lambda i: (i, 0))
    return pl.pallas_call(
        functools.partial(_outproj_kernel, tm, blocks_per_seq),
        out_shape=[jax.ShapeDtypeStruct((m, D_MODEL), F32), jax.ShapeDtypeStruct((m, D_MODEL), BF16)],
        grid=(m // tm,),
        in_specs=[
            row_spec(Q_COLS),
            row_spec(Q_COLS),
            row_spec(CONV_CH),
            pl.BlockSpec((BF16_SUBLANES, CONV_CH), lambda i: (jnp.maximum(i * halo - 1, 0), 0)),
            pl.BlockSpec((BF16_SUBLANES, CONV_CH), lambda i: (jnp.minimum((i + 1) * halo, last_halo), 0)),
            row_spec(CONV_CH),
            pl.BlockSpec(conv_w.shape, lambda i: (0, 0)),
            _resident((D_MODEL, D_MODEL), lambda i: (0, 0)),
            row_spec(D_MODEL),
            _mod_spec(2, row_of_block),
            _vec_spec(D_MODEL),
            _vec_spec(D_MODEL),
            _mod_spec(4, row_of_block),
            _mod_spec(3, row_of_block),
        ],
        out_specs=[row_spec(D_MODEL), row_spec(D_MODEL)],
        scratch_shapes=[pltpu.VMEM((tm, CONV_CH), BF16)],
        compiler_params=_params(("parallel",)),
        name="outproj",
    )(oa, ob, z, z, z, gp, conv_w, w_out, x, mod, g_post, g_ffn, mod, mod)


def _ffn_up_kernel(tm, round_down, h_ref, wg_ref, wu_ref, *rest):
    if round_down:
        wd_ref, a_ref, wd16_ref = rest
        wd16_ref[...] = wd_ref[...].astype(BF16)
    else:
        a_ref, = rest
    wg = wg_ref[...].astype(BF16)
    wu = wu_ref[...].astype(BF16)
    sub = min(tm, ROW_SUB)
    for r in range(tm // sub):
        rows = slice(r * sub, (r + 1) * sub)
        h = h_ref[rows, :]
        a_ref[rows, :] = (_silu(_dot(h, wg)) * _dot(h, wu)).astype(BF16)


def _ffn_down_kernel(tm, round_next, a_ref, wd_ref, x_ref, gt_ref, g_ref, *rest):
    if round_next:
        win_ref, wout_ref, o_ref, win16_ref, wout16_ref = rest
        win16_ref[...] = win_ref[...].astype(BF16)
        wout16_ref[...] = wout_ref[...].astype(BF16)
    else:
        o_ref, = rest
    gain = gt_ref[...] * g_ref[...]
    sub = min(tm, ROW_SUB)
    for r in range(tm // sub):
        rows = slice(r * sub, (r + 1) * sub)
        o_ref[rows, :] = x_ref[rows, :] + _rms(_dot(a_ref[rows, :], wd_ref[...])) * gain


def _ffn(h2, w_gate, w_up, w_down, layer, x, mod, row_of_block, g_post, tm_up, tm_down, w_down16=None,
         next_proj=None, tf=512):
    m = x.shape[0]
    hidden = w_gate.shape[-1]
    nf = hidden // tf
    round_down = w_down16 is None
    in_specs = [
        pl.BlockSpec((tm_up, D_MODEL), lambda i, f: (i, 0)),
        pl.BlockSpec((None, D_MODEL, tf), lambda i, f: (layer, 0, f)),
        pl.BlockSpec((None, D_MODEL, tf), lambda i, f: (layer, 0, f)),
    ]
    out_shape = [jax.ShapeDtypeStruct((m, hidden), BF16)]
    out_specs = [pl.BlockSpec((tm_up, tf), lambda i, f: (i, f))]
    args = [h2, w_gate, w_up]
    if round_down:
        slab = hidden // ((m // tm_up) * nf)
        assert slab % BF16_SUBLANES == 0 and slab * (m // tm_up) * nf == hidden
        in_specs.append(pl.BlockSpec((None, slab, D_MODEL), lambda i, f: (layer, i * nf + f, 0)))
        out_shape.append(jax.ShapeDtypeStruct((hidden, D_MODEL), BF16))
        out_specs.append(pl.BlockSpec((slab, D_MODEL), lambda i, f: (i * nf + f, 0)))
        args.append(w_down)
    outs = pl.pallas_call(
        functools.partial(_ffn_up_kernel, tm_up, round_down),
        out_shape=out_shape,
        grid=(m // tm_up, nf),
        in_specs=in_specs,
        out_specs=out_specs,
        compiler_params=_params(("parallel", "arbitrary")),
        name="ffn_up",
    )(*args)
    act = outs[0]
    if round_down:
        w_down16 = outs[1]
    in_specs = [
        pl.BlockSpec((tm_down, hidden), lambda i: (i, 0)),
        _resident((hidden, D_MODEL), lambda i: (0, 0)),
        pl.BlockSpec((tm_down, D_MODEL), lambda i: (i, 0)),
        _mod_spec(5, row_of_block),
        _vec_spec(D_MODEL),
    ]
    out_shape = [jax.ShapeDtypeStruct((m, D_MODEL), F32)]
    out_specs = [pl.BlockSpec((tm_down, D_MODEL), lambda i: (i, 0))]
    args = [act, w_down16, x, mod, g_post]
    if next_proj is not None:
        slab = D_MODEL // (m // tm_down)
        assert slab % BF16_SUBLANES == 0 and slab * (m // tm_down) == D_MODEL
        for w in next_proj:
            cols = w.shape[-1]
            in_specs.append(pl.BlockSpec((None, slab, cols), lambda i: (layer + 1, i, 0)))
            out_shape.append(jax.ShapeDtypeStruct((D_MODEL, cols), BF16))
            out_specs.append(pl.BlockSpec((slab, cols), lambda i: (i, 0)))
            args.append(w)
    outs = pl.pallas_call(
        functools.partial(_ffn_down_kernel, tm_down, next_proj is not None),
        out_shape=out_shape,
        grid=(m // tm_down,),
        in_specs=in_specs,
        out_specs=out_specs,
        compiler_params=_params(("parallel",)),
        name="ffn_down",
    )(*args)
    return outs[0], w_down16, tuple(outs[1:])


def _rope_tables(seq):
    pos = jnp.arange(seq, dtype=jnp.int32)
    inv_freq = jnp.power(ROPE_BASE, -jnp.arange(ROPE_PAIRS, dtype=F32) / ROPE_PAIRS)
    ang = jnp.stack([pos // GRID_W, pos % GRID_W], axis=1).astype(F32)[:, :, None] * inv_freq
    cos, sin = jnp.cos(ang), jnp.sin(ang)
    cos_t = jnp.concatenate([cos[:, 0], cos[:, 0], cos[:, 1], cos[:, 1]], axis=-1)
    sin_t = jnp.concatenate([-sin[:, 0], sin[:, 0], -sin[:, 1], sin[:, 1]], axis=-1)
    return cos_t, sin_t


def kernel(x, c, ctx, c_ctx, w_ada, b_ada, g_mix_pre, g_mix_post, g_ffn_pre, g_ffn_post, w_in, q_norm, k_norm,
           sink, conv_w, w_out, w_gate, w_up, w_down):
    batch, seq, d = x.shape
    ctx_len = ctx.shape[1]
    depth = w_ada.shape[0]
    assert d == D_MODEL and batch < MOD_ROWS and seq % 512 == 0 and ctx_len % Q_BLOCK == 0

    tm = 512
    tm_in = 1024
    tm_ctx = ctx_len
    tm_up = 2048
    tm_ctx_ffn = min(tm, batch * ctx_len)
    lat_row = lambda i: i // (seq // tm)
    lat_row_in = lambda i: i // (seq // tm_in)
    ctx_row = lambda i: batch

    cond = jnp.zeros((MOD_ROWS, d), F32).at[:batch].set(c).at[batch].set(c_ctx)
    mod_all = _ada(cond, w_ada, b_ada).reshape(depth, MOD_ROWS * N_MOD, 1, d)
    tables = _rope_tables(seq)
    w_in16, w_out16 = w_in[0].astype(BF16), w_out[0].astype(BF16)

    xl = x.reshape(batch * seq, d)
    xc = ctx.reshape(batch * ctx_len, d)
    vec = lambda a, l: a[l].reshape(1, -1)
    for l in range(depth):
        mod = mod_all[l]
        g_pre, g_post = vec(g_mix_pre, l), vec(g_mix_post, l)
        g_ffn, g_ffn_out = vec(g_ffn_pre, l), vec(g_ffn_post, l)
        qn, kn = vec(q_norm, l), vec(k_norm, l)
        update_ctx = l < depth - 1

        ctx_proj = _inproj(xc, mod, ctx_row, g_pre, w_in16, qn, kn, None, tm_ctx, ctx_len, kv_only=not update_ctx)
        kv_c = ctx_proj[0]
        kv, qa, qb, z, gp = _inproj(xl, mod, lat_row_in, g_pre, w_in16, qn, kn, tables, tm_in, seq)
        oa = _attn_a(qa, kv, kv_c, batch, seq, ctx_len)
        ob = _attn_b(qb, kv, kv_c, sink[l], batch, seq, ctx_len)
        xl, h2 = _outproj(oa, ob, z, gp, conv_w[l], w_out16, xl, mod, lat_row, g_post, g_ffn, tm, seq)
        xl, w_down16, next16 = _ffn(h2, w_gate, w_up, w_down, l, xl, mod, lat_row, g_ffn_out, tm_up, tm,
                                    next_proj=(w_in, w_out) if l + 1 < depth else None)
        if update_ctx:
            _, qa_c, qb_c, z_c, gp_c = ctx_proj
            oa_c, ob_c = _attn_ctx(qa_c, qb_c, kv_c, sink[l], batch, ctx_len)
            xc, h2_c = _outproj(oa_c, ob_c, z_c, gp_c, conv_w[l], w_out16, xc, mod, ctx_row, g_post, g_ffn,
                                tm_ctx, ctx_len)
            xc, _, _ = _ffn(h2_c, w_gate, w_up, w_down, l, xc, mod, ctx_row, g_ffn_out,
                            min(tm_up, batch * ctx_len), tm_ctx_ffn, w_down16)
        if l + 1 < depth:
            w_in16, w_out16 = next16
    return xl.reshape(batch, seq, d)
```

```python
import functools
from typing import NamedTuple

import jax
import jax.numpy as jnp
from jax import lax
from jax.experimental import pallas as pl
from jax.experimental.pallas import tpu as pltpu

F32 = jnp.float32
BF16 = jnp.bfloat16

D_MODEL = 2048
HEAD_DIM = 128
GRID_W = 64
A_Q_HEADS = 6
A_KV_HEADS = 2
B_Q_HEADS = 6
B_KV_HEADS = 2
GROUP = A_Q_HEADS // A_KV_HEADS
Q_COLS = A_Q_HEADS * HEAD_DIM
CONV_CH = D_MODEL - 2 * Q_COLS
WINDOW = 128
Q_BLOCK = 128
ROPE_BASE = 10000.0
ROPE_PAIRS = HEAD_DIM // 4
NORM_EPS = 1e-6
NEG_INF = -1e30
ATTN_SCALE = HEAD_DIM ** -0.5
LOG2_E = 1.4426950408889634
SCORE_SCALE = ATTN_SCALE * LOG2_E
N_MOD = 6
MOD_ROWS = 8

KV_COLS = 4 * A_KV_HEADS * HEAD_DIM
OFF_AQ = KV_COLS
OFF_BQ = OFF_AQ + Q_COLS
OFF_CONV = OFF_BQ + Q_COLS
IN_COLS = OFF_CONV + 3 * CONV_CH

V7X_VMEM_BYTES = 64 * 1024 * 1024
VMEM_LIMIT = V7X_VMEM_BYTES - 8 * 1024 * 1024
BF16_SUBLANES = 16


class _Tiles(NamedTuple):
    row: int = 512
    row_in: int = 1024
    row_up: int = 2048
    row_sub: int = 256
    row_sub_up: int = 128
    ffn_cols: int = 512
    ada_cols: int = 1024
    attn_q: int = 256
    attn_k: int = 512
    attn_a_blocks: int = 4
    attn_b_blocks: int = 4


TILES = _Tiles()


def _rms(t):
    return t * lax.rsqrt(jnp.mean(t * t, axis=-1, keepdims=True) + NORM_EPS)


def _silu(t):
    return t * (1.0 / (1.0 + jnp.exp(-t)))


def _dot(a, b):
    return jnp.dot(a, b, preferred_element_type=F32)


def _dot_nt(a, b):
    return lax.dot_general(a, b, (((1,), (1,)), ((), ())), preferred_element_type=F32)


def _params(semantics):
    return pltpu.CompilerParams(dimension_semantics=semantics, vmem_limit_bytes=VMEM_LIMIT)


def _resident(shape, index_map):
    return pl.BlockSpec(shape, index_map, pipeline_mode=pl.Buffered(1))


def _ada_kernel(s_ref, w_ref, b_ref, o_ref):
    a = _silu(s_ref[...]).astype(BF16)
    o_ref[0] = _dot(a, w_ref[0].astype(BF16)) + b_ref[0]


def _ada(cond, w_ada, b_ada, tn=TILES.ada_cols):
    depth, d, n = w_ada.shape
    return pl.pallas_call(
        _ada_kernel,
        out_shape=jax.ShapeDtypeStruct((depth, MOD_ROWS, n), F32),
        grid=(depth, n // tn),
        in_specs=[
            pl.BlockSpec((MOD_ROWS, d), lambda l, j: (0, 0)),
            pl.BlockSpec((1, d, tn), lambda l, j: (l, 0, j)),
            pl.BlockSpec((1, 1, tn), lambda l, j: (l, 0, j)),
        ],
        out_specs=pl.BlockSpec((1, MOD_ROWS, tn), lambda l, j: (l, 0, j)),
        compiler_params=_params(("parallel", "parallel")),
        name="ada",
    )(cond, w_ada, b_ada.reshape(depth, 1, n))


def _mod_spec(chunk, row_of_block):
    return pl.BlockSpec((None, 1, D_MODEL), lambda i: (row_of_block(i) * N_MOD + chunk, 0, 0))


def _vec_spec(n):
    return pl.BlockSpec((1, n), lambda i: (0, 0))


def _inproj_kernel(rope, kv_only, tm, x_ref, g_ref, sc_ref, sh_ref, w_ref, qn_ref, kn_ref, *rest):
    if rope:
        cos_ref, sin_ref = rest[:2]
        rest = rest[2:]
    if kv_only:
        kv_ref, = rest
    else:
        kv_ref, qa_ref, qb_ref, z_ref, gp_ref = rest
    sub = min(tm, TILES.row_sub)
    pre_gain = g_ref[...] * (1.0 + sc_ref[...])
    shift = sh_ref[...]
    k_gain = kn_ref[...] * SCORE_SCALE
    q_gain = qn_ref[...]
    lane = lax.broadcasted_iota(jnp.int32, (sub, HEAD_DIM), 1)
    first_half = (lane & ROPE_PAIRS) == 0

    for r in range(tm // sub):
        rows = slice(r * sub, (r + 1) * sub)
        if rope:
            cos = cos_ref[rows, :]
            sin = sin_ref[rows, :]

            def rot(t):
                partner = jnp.where(first_half, pltpu.roll(t, HEAD_DIM - ROPE_PAIRS, 1),
                                    pltpu.roll(t, ROPE_PAIRS, 1))
                return t * cos + partner * sin
        else:
            def rot(t):
                return t

        hb = (_rms(x_ref[rows, :]) * pre_gain + shift).astype(BF16)

        def proj(c0, n):
            return _dot(hb, w_ref[:, c0:c0 + n])

        def head(t, k):
            return t[:, k * HEAD_DIM:(k + 1) * HEAD_DIM]

        def put(ref, k, t):
            ref[rows, k * HEAD_DIM:(k + 1) * HEAD_DIM] = t.astype(BF16)

        pkv = proj(0, KV_COLS)
        for k in range(A_KV_HEADS):
            put(kv_ref, k, rot(_rms(head(pkv, k)) * k_gain))
            put(kv_ref, A_KV_HEADS + k, head(pkv, A_KV_HEADS + k))
            put(kv_ref, 2 * A_KV_HEADS + k, rot(head(pkv, 2 * A_KV_HEADS + k)) * SCORE_SCALE)
            put(kv_ref, 3 * A_KV_HEADS + k, head(pkv, 3 * A_KV_HEADS + k))
        if kv_only:
            continue

        pq = proj(OFF_AQ, Q_COLS)
        for k in range(A_Q_HEADS):
            put(qa_ref, k, rot(_rms(head(pq, k)) * q_gain))
        pq = proj(OFF_BQ, Q_COLS)
        for k in range(B_Q_HEADS):
            put(qb_ref, k, rot(head(pq, k)))

        z_ref[rows, :] = (proj(OFF_CONV, CONV_CH) * proj(OFF_CONV + 2 * CONV_CH, CONV_CH)).astype(BF16)
        gp_ref[rows, :] = proj(OFF_CONV + CONV_CH, CONV_CH).astype(BF16)


def _inproj(x, mod, row_of_block, g_pre, w_in, q_norm, k_norm, rope_tables, tm, seq_len, kv_only=False):
    m = x.shape[0]
    rope = rope_tables is not None
    cols = KV_COLS if kv_only else IN_COLS
    in_specs = [
        pl.BlockSpec((tm, D_MODEL), lambda i: (i, 0)),
        _vec_spec(D_MODEL),
        _mod_spec(1, row_of_block),
        _mod_spec(0, row_of_block),
        _resident((D_MODEL, cols), lambda i: (0, 0)),
        _vec_spec(HEAD_DIM),
        _vec_spec(HEAD_DIM),
    ]
    args = [x, g_pre, mod, mod, w_in, q_norm, k_norm]
    if rope:
        blocks_per_seq = seq_len // tm
        in_specs += [pl.BlockSpec((tm, HEAD_DIM), lambda i: (i % blocks_per_seq, 0))] * 2
        args += list(rope_tables)
    widths = (KV_COLS,) if kv_only else (KV_COLS, Q_COLS, Q_COLS, CONV_CH, CONV_CH)
    return pl.pallas_call(
        functools.partial(_inproj_kernel, rope, kv_only, tm),
        out_shape=[jax.ShapeDtypeStruct((m, w), BF16) for w in widths],
        grid=(m // tm,),
        in_specs=in_specs,
        out_specs=[pl.BlockSpec((tm, w), lambda i: (i, 0)) for w in widths],
        compiler_params=_params(("parallel",)),
        name=("inproj_rope" if rope else "inproj") + ("_kv" if kv_only else ""),
    )(*args)


def _stack_heads(q_ref, kvh, rows=slice(None)):
    base = kvh * GROUP * HEAD_DIM
    return jnp.concatenate(
        [q_ref[rows, base + g * HEAD_DIM: base + (g + 1) * HEAD_DIM] for g in range(GROUP)], axis=0)


def _unstack_heads(o_ref, kvh, o, tq, rows=slice(None)):
    base = kvh * GROUP * HEAD_DIM
    for g in range(GROUP):
        o_ref[rows, base + g * HEAD_DIM: base + (g + 1) * HEAD_DIM] = o[g * tq:(g + 1) * tq].astype(BF16)


def _sink_column(sink_ref, kvh, tq):
    grp = lax.broadcasted_iota(jnp.int32, (GROUP * tq, 1), 0) // tq
    col = jnp.full((GROUP * tq, 1), sink_ref[kvh * GROUP], F32)
    for g in range(1, GROUP):
        col = jnp.where(grp == g, sink_ref[kvh * GROUP + g], col)
    return col * LOG2_E


def _softmax_pv(scores, values, sink_col=None):
    tiles = [s[:, t * HEAD_DIM:(t + 1) * HEAD_DIM] for s in scores for t in range(s.shape[1] // HEAD_DIM)]
    m = functools.reduce(jnp.maximum, tiles).max(axis=-1, keepdims=True)
    if sink_col is not None:
        m = jnp.maximum(m, sink_col)
    acc = None
    for s, v in zip(scores, values):
        v1 = jnp.concatenate([v, jnp.ones_like(v)], axis=1)
        a = _dot(jnp.exp2(s - m).astype(BF16), v1)
        acc = a if acc is None else acc + a
    den = acc[:, HEAD_DIM:]
    if sink_col is not None:
        den = den + jnp.exp2(sink_col - m)
    return acc[:, :HEAD_DIM] / den


def _values_with_ones(v):
    return jnp.concatenate([v.T, jnp.ones((BF16_SUBLANES, v.shape[0]), BF16)], axis=0)


def _attn_a_kernel(tq, tk, nblk, q_ref, kc_ref, vc_ref, k_ref, v_ref, o_ref, acc_scr):
    nc = k_ref.shape[0] // tk

    def scores(stage):
        j, c = stage
        k, v = (kc_ref[...], vc_ref[...]) if c < 0 else (k_ref[c * tk:(c + 1) * tk, :], v_ref[c * tk:(c + 1) * tk, :])
        return _dot_nt(k, qs[j]), v

    stages = [(j, c) for j in range(nblk) for c in range(-1, nc)]
    qs = [_stack_heads(q_ref, 0, slice(j * tq, (j + 1) * tq)) for j in range(nblk)]
    m = None
    s, v = scores(stages[0])
    for n, (j, c) in enumerate(stages):
        if n + 1 < len(stages):
            s_next, v_next = scores(stages[n + 1])
        acc = acc_scr.at[j]
        m_new = s.max(axis=0, keepdims=True)
        if c >= 0:
            m_new = jnp.maximum(m, m_new)
        pv = _dot(_values_with_ones(v), jnp.exp2(s - m_new).astype(BF16))
        if c < 0:
            acc[...] = pv
        else:
            acc[...] = jnp.exp2(m - m_new) * acc[...] + pv
        m = m_new
        if c == nc - 1:
            a = acc[...]
            o = a[:HEAD_DIM] / a[HEAD_DIM:HEAD_DIM + 1]
            for g in range(GROUP):
                o_ref[j * tq:(j + 1) * tq, g * HEAD_DIM:(g + 1) * HEAD_DIM] = (
                    o[:, g * tq:(g + 1) * tq].T.astype(BF16))
        if n + 1 < len(stages):
            s, v = s_next, v_next


def _attn_a(qa, kv, kv_ctx, batch, seq, ctx_len, tq=TILES.attn_q, tk=TILES.attn_k, nblk=TILES.attn_a_blocks):
    nq = seq // (tq * nblk)
    gw = GROUP * HEAD_DIM
    return pl.pallas_call(
        functools.partial(_attn_a_kernel, tq, tk, nblk),
        out_shape=jax.ShapeDtypeStruct(qa.shape, BF16),
        grid=(batch, A_KV_HEADS, nq),
        in_specs=[
            pl.BlockSpec((tq * nblk, gw), lambda b, h, i: (b * nq + i, h)),
            pl.BlockSpec((ctx_len, HEAD_DIM), lambda b, h, i: (b, h)),
            pl.BlockSpec((ctx_len, HEAD_DIM), lambda b, h, i: (b, A_KV_HEADS + h)),
            pl.BlockSpec((seq, HEAD_DIM), lambda b, h, i: (b, h)),
            pl.BlockSpec((seq, HEAD_DIM), lambda b, h, i: (b, A_KV_HEADS + h)),
        ],
        out_specs=pl.BlockSpec((tq * nblk, gw), lambda b, h, i: (b * nq + i, h)),
        scratch_shapes=[pltpu.VMEM((nblk, HEAD_DIM + BF16_SUBLANES, GROUP * tq), F32)],
        compiler_params=_params(("parallel", "parallel", "arbitrary")),
        name="attn_a",
    )(qa, kv_ctx, kv_ctx, kv, kv)


def _attn_b_kernel(seq, nblk, sink_ref, q_ref, kvc_ref, kv_ref, o_ref):
    span = Q_BLOCK + 2 * WINDOW
    rows = GROUP * Q_BLOCK
    row_q = lax.broadcasted_iota(jnp.int32, (rows, span), 0) % Q_BLOCK
    col_k = lax.broadcasted_iota(jnp.int32, (rows, span), 1)
    work = []
    for j in range(nblk):
        blk = pl.program_id(1) * nblk + j
        qrows = slice(j * Q_BLOCK, (j + 1) * Q_BLOCK)
        start = pl.multiple_of(jnp.clip(blk * Q_BLOCK - WINDOW, 0, seq - span), Q_BLOCK)
        valid = jnp.abs(blk * Q_BLOCK - start + row_q - col_k) <= WINDOW
        for kvh in range(B_KV_HEADS):
            kc = slice(kvh * HEAD_DIM, (kvh + 1) * HEAD_DIM)
            q = _stack_heads(q_ref, kvh, qrows)
            s_win = jnp.where(valid, _dot_nt(q, kv_ref[pl.ds(start, span), kc]), NEG_INF)
            s_ctx = _dot_nt(q, kvc_ref[:, kc])
            work.append((qrows, kvh, start, s_win, s_ctx))
    for qrows, kvh, start, s_win, s_ctx in work:
        vc = slice((B_KV_HEADS + kvh) * HEAD_DIM, (B_KV_HEADS + kvh + 1) * HEAD_DIM)
        o = _softmax_pv([s_win, s_ctx], [kv_ref[pl.ds(start, span), vc], kvc_ref[:, vc]],
                        _sink_column(sink_ref, kvh, Q_BLOCK))
        _unstack_heads(o_ref, kvh, o, Q_BLOCK, qrows)


def _attn_b(qb, kv, kv_ctx, sink, batch, seq, ctx_len, nblk=TILES.attn_b_blocks):
    nq = seq // (Q_BLOCK * nblk)
    bw = 2 * B_KV_HEADS * HEAD_DIM
    return pl.pallas_call(
        functools.partial(_attn_b_kernel, seq, nblk),
        out_shape=jax.ShapeDtypeStruct(qb.shape, BF16),
        grid_spec=pltpu.PrefetchScalarGridSpec(
            num_scalar_prefetch=1,
            grid=(batch, nq),
            in_specs=[
                pl.BlockSpec((Q_BLOCK * nblk, Q_COLS), lambda b, i, s: (b * nq + i, 0)),
                pl.BlockSpec((ctx_len, bw), lambda b, i, s: (b, 1)),
                pl.BlockSpec((seq, bw), lambda b, i, s: (b, 1)),
            ],
            out_specs=pl.BlockSpec((Q_BLOCK * nblk, Q_COLS), lambda b, i, s: (b * nq + i, 0)),
        ),
        compiler_params=_params(("parallel", "arbitrary")),
        name="attn_b",
    )(sink, qb, kv_ctx, kv)


def _attn_ctx_kernel(ctx_len, sink_ref, qa_ref, qb_ref, kv_ref, oa_ref, ob_ref):
    for kvh in range(A_KV_HEADS):
        k = kv_ref[:, kvh * HEAD_DIM:(kvh + 1) * HEAD_DIM]
        v = kv_ref[:, (A_KV_HEADS + kvh) * HEAD_DIM:(A_KV_HEADS + kvh + 1) * HEAD_DIM]
        o = _softmax_pv([_dot_nt(_stack_heads(qa_ref, kvh), k)], [v])
        _unstack_heads(oa_ref, kvh, o, ctx_len)
    off = 2 * A_KV_HEADS
    for kvh in range(B_KV_HEADS):
        k = kv_ref[:, (off + kvh) * HEAD_DIM:(off + kvh + 1) * HEAD_DIM]
        v = kv_ref[:, (off + B_KV_HEADS + kvh) * HEAD_DIM:(off + B_KV_HEADS + kvh + 1) * HEAD_DIM]
        o = _softmax_pv([_dot_nt(_stack_heads(qb_ref, kvh), k)], [v], _sink_column(sink_ref, kvh, ctx_len))
        _unstack_heads(ob_ref, kvh, o, ctx_len)


def _attn_ctx(qa, qb, kv_ctx, sink, batch, ctx_len):
    q_spec = pl.BlockSpec((ctx_len, Q_COLS), lambda b, s: (b, 0))
    return pl.pallas_call(
        functools.partial(_attn_ctx_kernel, ctx_len),
        out_shape=[jax.ShapeDtypeStruct(qa.shape, BF16), jax.ShapeDtypeStruct(qb.shape, BF16)],
        grid_spec=pltpu.PrefetchScalarGridSpec(
            num_scalar_prefetch=1,
            grid=(batch,),
            in_specs=[q_spec, q_spec, pl.BlockSpec((ctx_len, KV_COLS), lambda b, s: (b, 0))],
            out_specs=[q_spec, q_spec],
        ),
        compiler_params=_params(("parallel",)),
        name="attn_ctx",
    )(sink, qa, qb, kv_ctx)


def _outproj_kernel(tm, blocks_per_seq, oa_ref, ob_ref, z_ref, zprev_ref, znext_ref, gp_ref, cw_ref, w_ref,
                    x_ref, gt_ref, gpost_ref, gffn_ref, sc_ref, sh_ref, xo_ref, h2_ref, oc_scr):
    i = pl.program_id(0)
    pos = i % blocks_per_seq
    z = z_ref[...].astype(F32)
    prev_row = jnp.where(pos == 0, 0.0, zprev_ref[BF16_SUBLANES - 1:BF16_SUBLANES, :].astype(F32))
    next_row = jnp.where(pos == blocks_per_seq - 1, 0.0, znext_ref[0:1, :].astype(F32))
    row = lax.broadcasted_iota(jnp.int32, z.shape, 0)
    z_prev = jnp.where(row == 0, prev_row, pltpu.roll(z, 1, 0))
    z_next = jnp.where(row == tm - 1, next_row, pltpu.roll(z, tm - 1, 0))
    cw = cw_ref[...]
    oc = gp_ref[...].astype(F32) * (z_prev * cw[0:1] + z * cw[1:2] + z_next * cw[2:3])
    oc_scr[...] = oc.astype(BF16)

    post_gain = gt_ref[...] * gpost_ref[...]
    ffn_gain = gffn_ref[...] * (1.0 + sc_ref[...])
    shift = sh_ref[...]
    sub = min(tm, TILES.row_sub)
    for r in range(tm // sub):
        rows = slice(r * sub, (r + 1) * sub)
        y = _dot(oa_ref[rows, :], w_ref[0:Q_COLS, :])
        y += _dot(ob_ref[rows, :], w_ref[Q_COLS:2 * Q_COLS, :])
        y += _dot(oc_scr[rows, :], w_ref[2 * Q_COLS:, :])
        x = x_ref[rows, :] + _rms(y) * post_gain
        xo_ref[rows, :] = x
        h2_ref[rows, :] = (_rms(x) * ffn_gain + shift).astype(BF16)


def _outproj(oa, ob, z, gp, conv_w, w_out, x, mod, row_of_block, g_post, g_ffn, tm, seq_len):
    m = x.shape[0]
    blocks_per_seq = seq_len // tm
    halo = tm // BF16_SUBLANES
    last_halo = m // BF16_SUBLANES - 1
    row_spec = lambda w: pl.BlockSpec((tm, w), lambda i: (i, 0))
    return pl.pallas_call(
        functools.partial(_outproj_kernel, tm, blocks_per_seq),
        out_shape=[jax.ShapeDtypeStruct((m, D_MODEL), F32), jax.ShapeDtypeStruct((m, D_MODEL), BF16)],
        grid=(m // tm,),
        in_specs=[
            row_spec(Q_COLS),
            row_spec(Q_COLS),
            row_spec(CONV_CH),
            pl.BlockSpec((BF16_SUBLANES, CONV_CH), lambda i: (jnp.maximum(i * halo - 1, 0), 0)),
            pl.BlockSpec((BF16_SUBLANES, CONV_CH), lambda i: (jnp.minimum((i + 1) * halo, last_halo), 0)),
            row_spec(CONV_CH),
            pl.BlockSpec(conv_w.shape, lambda i: (0, 0)),
            _resident((D_MODEL, D_MODEL), lambda i: (0, 0)),
            row_spec(D_MODEL),
            _mod_spec(2, row_of_block),
            _vec_spec(D_MODEL),
            _vec_spec(D_MODEL),
            _mod_spec(4, row_of_block),
            _mod_spec(3, row_of_block),
        ],
        out_specs=[row_spec(D_MODEL), row_spec(D_MODEL)],
        scratch_shapes=[pltpu.VMEM((tm, CONV_CH), BF16)],
        compiler_params=_params(("parallel",)),
        name="outproj",
    )(oa, ob, z, z, z, gp, conv_w, w_out, x, mod, g_post, g_ffn, mod, mod)


def _ffn_up_kernel(tm, round_down, h_ref, wg_ref, wu_ref, *rest):
    if round_down:
        wd_ref, a_ref, wd16_ref = rest
        wd16_ref[...] = wd_ref[...].astype(BF16)
    else:
        a_ref, = rest
    wg = wg_ref[...].astype(BF16)
    wu = wu_ref[...].astype(BF16)
    sub = min(tm, TILES.row_sub_up)
    for r in range(tm // sub):
        rows = slice(r * sub, (r + 1) * sub)
        h = h_ref[rows, :]
        a_ref[rows, :] = (_silu(_dot(h, wg)) * _dot(h, wu)).astype(BF16)


def _ffn_down_kernel(tm, round_next, a_ref, wd_ref, x_ref, gt_ref, g_ref, *rest):
    if round_next:
        win_ref, wout_ref, o_ref, win16_ref, wout16_ref = rest
        win16_ref[...] = win_ref[...].astype(BF16)
        wout16_ref[...] = wout_ref[...].astype(BF16)
    else:
        o_ref, = rest
    gain = gt_ref[...] * g_ref[...]
    sub = min(tm, TILES.row_sub)
    for r in range(tm // sub):
        rows = slice(r * sub, (r + 1) * sub)
        o_ref[rows, :] = x_ref[rows, :] + _rms(_dot(a_ref[rows, :], wd_ref[...])) * gain


def _ffn(h2, w_gate, w_up, w_down, layer, x, mod, row_of_block, g_post, tm_up, tm_down, w_down16=None,
         next_proj=None, tf=TILES.ffn_cols):
    m = x.shape[0]
    hidden = w_gate.shape[-1]
    nf = hidden // tf
    round_down = w_down16 is None
    in_specs = [
        pl.BlockSpec((tm_up, D_MODEL), lambda i, f: (i, 0)),
        pl.BlockSpec((None, D_MODEL, tf), lambda i, f: (layer, 0, f)),
        pl.BlockSpec((None, D_MODEL, tf), lambda i, f: (layer, 0, f)),
    ]
    out_shape = [jax.ShapeDtypeStruct((m, hidden), BF16)]
    out_specs = [pl.BlockSpec((tm_up, tf), lambda i, f: (i, f))]
    args = [h2, w_gate, w_up]
    if round_down:
        slab = hidden // ((m // tm_up) * nf)
        assert slab % BF16_SUBLANES == 0 and slab * (m // tm_up) * nf == hidden
        in_specs.append(pl.BlockSpec((None, slab, D_MODEL), lambda i, f: (layer, i * nf + f, 0)))
        out_shape.append(jax.ShapeDtypeStruct((hidden, D_MODEL), BF16))
        out_specs.append(pl.BlockSpec((slab, D_MODEL), lambda i, f: (i * nf + f, 0)))
        args.append(w_down)
    outs = pl.pallas_call(
        functools.partial(_ffn_up_kernel, tm_up, round_down),
        out_shape=out_shape,
        grid=(m // tm_up, nf),
        in_specs=in_specs,
        out_specs=out_specs,
        compiler_params=_params(("parallel", "arbitrary")),
        name="ffn_up",
    )(*args)
    act = outs[0]
    if round_down:
        w_down16 = outs[1]
    in_specs = [
        pl.BlockSpec((tm_down, hidden), lambda i: (i, 0)),
        _resident((hidden, D_MODEL), lambda i: (0, 0)),
        pl.BlockSpec((tm_down, D_MODEL), lambda i: (i, 0)),
        _mod_spec(5, row_of_block),
        _vec_spec(D_MODEL),
    ]
    out_shape = [jax.ShapeDtypeStruct((m, D_MODEL), F32)]
    out_specs = [pl.BlockSpec((tm_down, D_MODEL), lambda i: (i, 0))]
    args = [act, w_down16, x, mod, g_post]
    if next_proj is not None:
        slab = D_MODEL // (m // tm_down)
        assert slab % BF16_SUBLANES == 0 and slab * (m // tm_down) == D_MODEL
        for w in next_proj:
            cols = w.shape[-1]
            in_specs.append(pl.BlockSpec((None, slab, cols), lambda i: (layer + 1, i, 0)))
            out_shape.append(jax.ShapeDtypeStruct((D_MODEL, cols), BF16))
            out_specs.append(pl.BlockSpec((slab, cols), lambda i: (i, 0)))
            args.append(w)
    outs = pl.pallas_call(
        functools.partial(_ffn_down_kernel, tm_down, next_proj is not None),
        out_shape=out_shape,
        grid=(m // tm_down,),
        in_specs=in_specs,
        out_specs=out_specs,
        compiler_params=_params(("parallel",)),
        name="ffn_down",
    )(*args)
    return outs[0], w_down16, tuple(outs[1:])


def _rope_tables(seq):
    rows = seq // GRID_W
    inv_freq = jnp.power(ROPE_BASE, -jnp.arange(ROPE_PAIRS, dtype=F32) / ROPE_PAIRS)
    ang_row = jnp.arange(rows, dtype=jnp.int32).astype(F32)[:, None] * inv_freq
    ang_col = jnp.arange(GRID_W, dtype=jnp.int32).astype(F32)[:, None] * inv_freq
    by_row = lambda t: jnp.repeat(t, GRID_W, axis=0)
    by_col = lambda t: jnp.tile(t, (rows, 1))
    cos_r, sin_r = by_row(jnp.cos(ang_row)), by_row(jnp.sin(ang_row))
    cos_c, sin_c = by_col(jnp.cos(ang_col)), by_col(jnp.sin(ang_col))
    cos_t = jnp.concatenate([cos_r, cos_r, cos_c, cos_c], axis=-1)
    sin_t = jnp.concatenate([-sin_r, sin_r, -sin_c, sin_c], axis=-1)
    return cos_t, sin_t


def kernel(x, c, ctx, c_ctx, w_ada, b_ada, g_mix_pre, g_mix_post, g_ffn_pre, g_ffn_post, w_in, q_norm, k_norm,
           sink, conv_w, w_out, w_gate, w_up, w_down):
    batch, seq, d = x.shape
    ctx_len = ctx.shape[1]
    depth = w_ada.shape[0]
    assert d == D_MODEL and batch < MOD_ROWS and ctx_len % Q_BLOCK == 0
    assert seq % TILES.row_in == 0 and (batch * seq) % TILES.row_up == 0 and seq % GRID_W == 0

    tm, tm_in, tm_up = TILES.row, TILES.row_in, TILES.row_up
    tm_ctx = ctx_len
    tm_ctx_ffn = min(tm, batch * ctx_len)
    lat_row = lambda i: i // (seq // tm)
    lat_row_in = lambda i: i // (seq // tm_in)
    ctx_row = lambda i: batch

    cond = jnp.zeros((MOD_ROWS, d), F32).at[:batch].set(c).at[batch].set(c_ctx)
    mod_all = _ada(cond, w_ada, b_ada).reshape(depth, MOD_ROWS * N_MOD, 1, d)
    tables = _rope_tables(seq)
    w_in16, w_out16 = w_in[0].astype(BF16), w_out[0].astype(BF16)

    xl = x.reshape(batch * seq, d)
    xc = ctx.reshape(batch * ctx_len, d)
    vec = lambda a, l: a[l].reshape(1, -1)
    for l in range(depth):
        mod = mod_all[l]
        g_pre, g_post = vec(g_mix_pre, l), vec(g_mix_post, l)
        g_ffn, g_ffn_out = vec(g_ffn_pre, l), vec(g_ffn_post, l)
        qn, kn = vec(q_norm, l), vec(k_norm, l)
        update_ctx = l < depth - 1

        ctx_proj = _inproj(xc, mod, ctx_row, g_pre, w_in16, qn, kn, None, tm_ctx, ctx_len, kv_only=not update_ctx)
        kv_c = ctx_proj[0]
        kv, qa, qb, z, gp = _inproj(xl, mod, lat_row_in, g_pre, w_in16, qn, kn, tables, tm_in, seq)
        oa = _attn_a(qa, kv, kv_c, batch, seq, ctx_len)
        ob = _attn_b(qb, kv, kv_c, sink[l], batch, seq, ctx_len)
        xl, h2 = _outproj(oa, ob, z, gp, conv_w[l], w_out16, xl, mod, lat_row, g_post, g_ffn, tm, seq)
        xl, w_down16, next16 = _ffn(h2, w_gate, w_up, w_down, l, xl, mod, lat_row, g_ffn_out, tm_up, tm,
                                    next_proj=(w_in, w_out) if l + 1 < depth else None)
        if update_ctx:
            _, qa_c, qb_c, z_c, gp_c = ctx_proj
            oa_c, ob_c = _attn_ctx(qa_c, qb_c, kv_c, sink[l], batch, ctx_len)
            xc, h2_c = _outproj(oa_c, ob_c, z_c, gp_c, conv_w[l], w_out16, xc, mod, ctx_row, g_post, g_ffn,
                                tm_ctx, ctx_len)
            xc, _, _ = _ffn(h2_c, w_gate, w_up, w_down, l, xc, mod, ctx_row, g_ffn_out,
                            min(tm_up, batch * ctx_len), tm_ctx_ffn, w_down16)
        if l + 1 < depth:
            w_in16, w_out16 = next16
    return xl.reshape(batch, seq, d)
```

```python
import functools
from typing import NamedTuple

import jax
import jax.numpy as jnp
from jax import lax
from jax.experimental import pallas as pl
from jax.experimental.pallas import tpu as pltpu

F32 = jnp.float32
BF16 = jnp.bfloat16

D_MODEL = 2048
HEAD_DIM = 128
GRID_W = 64
A_Q_HEADS = 6
A_KV_HEADS = 2
B_Q_HEADS = 6
B_KV_HEADS = 2
GROUP = A_Q_HEADS // A_KV_HEADS
Q_COLS = A_Q_HEADS * HEAD_DIM
CONV_CH = D_MODEL - 2 * Q_COLS
WINDOW = 128
Q_BLOCK = 128
ROPE_BASE = 10000.0
ROPE_PAIRS = HEAD_DIM // 4
NORM_EPS = 1e-6
NEG_INF = -1e30
ATTN_SCALE = HEAD_DIM ** -0.5
LOG2_E = 1.4426950408889634
SCORE_SCALE = ATTN_SCALE * LOG2_E
N_MOD = 6
MOD_ROWS = 8

KV_COLS = 4 * A_KV_HEADS * HEAD_DIM
OFF_AQ = KV_COLS
OFF_BQ = OFF_AQ + Q_COLS
OFF_CONV = OFF_BQ + Q_COLS
IN_COLS = OFF_CONV + 3 * CONV_CH

V7X_VMEM_BYTES = 64 * 1024 * 1024
VMEM_LIMIT = V7X_VMEM_BYTES - 8 * 1024 * 1024
BF16_SUBLANES = 16


class _Tiles(NamedTuple):
    row: int = 512
    row_in: int = 1024
    row_up: int = 2048
    row_sub: int = 256
    row_sub_up: int = 128
    ffn_cols: int = 512
    ada_cols: int = 1024
    attn_q: int = 256
    attn_k: int = 512
    attn_a_blocks: int = 4
    attn_b_blocks: int = 4


TILES = _Tiles()


def _rms(t):
    return t * lax.rsqrt(jnp.mean(t * t, axis=-1, keepdims=True) + NORM_EPS)


def _silu(t):
    return t * (1.0 / (1.0 + jnp.exp(-t)))


def _dot(a, b):
    return jnp.dot(a, b, preferred_element_type=F32)


def _dot_nt(a, b):
    return lax.dot_general(a, b, (((1,), (1,)), ((), ())), preferred_element_type=F32)


def _params(semantics):
    return pltpu.CompilerParams(dimension_semantics=semantics, vmem_limit_bytes=VMEM_LIMIT)


def _resident(shape, index_map):
    return pl.BlockSpec(shape, index_map, pipeline_mode=pl.Buffered(1))


def _ada_slab(s_ref, w_ref, b_ref):
    return _dot(_silu(s_ref[...]).astype(BF16), w_ref[...].astype(BF16)) + b_ref[...]


def _ada_kernel(s_ref, w_ref, b_ref, o_ref):
    o_ref[0] = _ada_slab(s_ref, w_ref.at[0], b_ref.at[0])


def _ada(cond, w_ada, b_ada, layers, tn=TILES.ada_cols):
    _, d, n = w_ada.shape
    return pl.pallas_call(
        _ada_kernel,
        out_shape=jax.ShapeDtypeStruct((layers, MOD_ROWS, n), F32),
        grid=(layers, n // tn),
        in_specs=[
            pl.BlockSpec((MOD_ROWS, d), lambda l, j: (0, 0)),
            pl.BlockSpec((1, d, tn), lambda l, j: (l, 0, j)),
            pl.BlockSpec((1, 1, tn), lambda l, j: (l, 0, j)),
        ],
        out_specs=pl.BlockSpec((1, MOD_ROWS, tn), lambda l, j: (l, 0, j)),
        compiler_params=_params(("parallel", "parallel")),
        name="ada",
    )(cond, w_ada, b_ada.reshape(-1, 1, n))


def _mod_spec(chunk, row_of_block):
    return pl.BlockSpec((None, 1, D_MODEL), lambda i: (row_of_block(i) * N_MOD + chunk, 0, 0))


def _vec_spec(n):
    return pl.BlockSpec((1, n), lambda i: (0, 0))


def _inproj_kernel(rope, kv_only, tm, x_ref, g_ref, sc_ref, sh_ref, w_ref, qn_ref, kn_ref, *rest):
    if rope:
        cos_ref, sin_ref = rest[:2]
        rest = rest[2:]
    if kv_only:
        kv_ref, = rest
    else:
        kv_ref, qa_ref, qb_ref, z_ref, gp_ref = rest
    sub = min(tm, TILES.row_sub)
    pre_gain = g_ref[...] * (1.0 + sc_ref[...])
    shift = sh_ref[...]
    k_gain = kn_ref[...] * SCORE_SCALE
    q_gain = qn_ref[...]
    lane = lax.broadcasted_iota(jnp.int32, (sub, HEAD_DIM), 1)
    first_half = (lane & ROPE_PAIRS) == 0

    for r in range(tm // sub):
        rows = slice(r * sub, (r + 1) * sub)
        if rope:
            cos = cos_ref[rows, :]
            sin = sin_ref[rows, :]

            def rot(t):
                partner = jnp.where(first_half, pltpu.roll(t, HEAD_DIM - ROPE_PAIRS, 1),
                                    pltpu.roll(t, ROPE_PAIRS, 1))
                return t * cos + partner * sin
        else:
            def rot(t):
                return t

        hb = (_rms(x_ref[rows, :]) * pre_gain + shift).astype(BF16)

        def proj(c0, n):
            return _dot(hb, w_ref[:, c0:c0 + n])

        def head(t, k):
            return t[:, k * HEAD_DIM:(k + 1) * HEAD_DIM]

        def put(ref, k, t):
            ref[rows, k * HEAD_DIM:(k + 1) * HEAD_DIM] = t.astype(BF16)

        pkv = proj(0, KV_COLS)
        for k in range(A_KV_HEADS):
            put(kv_ref, k, rot(_rms(head(pkv, k)) * k_gain))
            put(kv_ref, A_KV_HEADS + k, head(pkv, A_KV_HEADS + k))
            put(kv_ref, 2 * A_KV_HEADS + k, rot(head(pkv, 2 * A_KV_HEADS + k)) * SCORE_SCALE)
            put(kv_ref, 3 * A_KV_HEADS + k, head(pkv, 3 * A_KV_HEADS + k))
        if kv_only:
            continue

        pq = proj(OFF_AQ, Q_COLS)
        for k in range(A_Q_HEADS):
            put(qa_ref, k, rot(_rms(head(pq, k)) * q_gain))
        pq = proj(OFF_BQ, Q_COLS)
        for k in range(B_Q_HEADS):
            put(qb_ref, k, rot(head(pq, k)))

        z_ref[rows, :] = (proj(OFF_CONV, CONV_CH) * proj(OFF_CONV + 2 * CONV_CH, CONV_CH)).astype(BF16)
        gp_ref[rows, :] = proj(OFF_CONV + CONV_CH, CONV_CH).astype(BF16)


def _inproj(x, mod, row_of_block, g_pre, w_in, q_norm, k_norm, rope_tables, tm, seq_len, kv_only=False):
    m = x.shape[0]
    rope = rope_tables is not None
    cols = KV_COLS if kv_only else IN_COLS
    in_specs = [
        pl.BlockSpec((tm, D_MODEL), lambda i: (i, 0)),
        _vec_spec(D_MODEL),
        _mod_spec(1, row_of_block),
        _mod_spec(0, row_of_block),
        _resident((D_MODEL, cols), lambda i: (0, 0)),
        _vec_spec(HEAD_DIM),
        _vec_spec(HEAD_DIM),
    ]
    args = [x, g_pre, mod, mod, w_in, q_norm, k_norm]
    if rope:
        blocks_per_seq = seq_len // tm
        in_specs += [pl.BlockSpec((tm, HEAD_DIM), lambda i: (i % blocks_per_seq, 0))] * 2
        args += list(rope_tables)
    widths = (KV_COLS,) if kv_only else (KV_COLS, Q_COLS, Q_COLS, CONV_CH, CONV_CH)
    return pl.pallas_call(
        functools.partial(_inproj_kernel, rope, kv_only, tm),
        out_shape=[jax.ShapeDtypeStruct((m, w), BF16) for w in widths],
        grid=(m // tm,),
        in_specs=in_specs,
        out_specs=[pl.BlockSpec((tm, w), lambda i: (i, 0)) for w in widths],
        compiler_params=_params(("parallel",)),
        name=("inproj_rope" if rope else "inproj") + ("_kv" if kv_only else ""),
    )(*args)


def _stack_heads(q_ref, kvh, rows=slice(None)):
    base = kvh * GROUP * HEAD_DIM
    return jnp.concatenate(
        [q_ref[rows, base + g * HEAD_DIM: base + (g + 1) * HEAD_DIM] for g in range(GROUP)], axis=0)


def _unstack_heads(o_ref, kvh, o, tq, rows=slice(None)):
    base = kvh * GROUP * HEAD_DIM
    for g in range(GROUP):
        o_ref[rows, base + g * HEAD_DIM: base + (g + 1) * HEAD_DIM] = o[g * tq:(g + 1) * tq].astype(BF16)


def _sink_column(sink_ref, kvh, tq):
    grp = lax.broadcasted_iota(jnp.int32, (GROUP * tq, 1), 0) // tq
    col = jnp.full((GROUP * tq, 1), sink_ref[kvh * GROUP], F32)
    for g in range(1, GROUP):
        col = jnp.where(grp == g, sink_ref[kvh * GROUP + g], col)
    return col * LOG2_E


def _softmax_pv(scores, values, sink_col=None):
    tiles = [s[:, t * HEAD_DIM:(t + 1) * HEAD_DIM] for s in scores for t in range(s.shape[1] // HEAD_DIM)]
    m = functools.reduce(jnp.maximum, tiles).max(axis=-1, keepdims=True)
    if sink_col is not None:
        m = jnp.maximum(m, sink_col)
    acc = None
    for s, v in zip(scores, values):
        v1 = jnp.concatenate([v, jnp.ones_like(v)], axis=1)
        a = _dot(jnp.exp2(s - m).astype(BF16), v1)
        acc = a if acc is None else acc + a
    den = acc[:, HEAD_DIM:]
    if sink_col is not None:
        den = den + jnp.exp2(sink_col - m)
    return acc[:, :HEAD_DIM] / den


def _values_with_ones(v):
    return jnp.concatenate([v.T, jnp.ones((BF16_SUBLANES, v.shape[0]), BF16)], axis=0)


def _attn_a_kernel(tq, tk, nblk, ada_next, q_ref, kc_ref, vc_ref, k_ref, v_ref, *rest):
    if ada_next:
        cond_ref, wa_ref, ba_ref, o_ref, mod_ref, acc_scr = rest
        mod_ref[...] = _ada_slab(cond_ref, wa_ref, ba_ref)
    else:
        o_ref, acc_scr = rest
    nc = k_ref.shape[0] // tk

    def scores(stage):
        j, c = stage
        k, v = (kc_ref[...], vc_ref[...]) if c < 0 else (k_ref[c * tk:(c + 1) * tk, :], v_ref[c * tk:(c + 1) * tk, :])
        return _dot_nt(k, qs[j]), v

    stages = [(j, c) for j in range(nblk) for c in range(-1, nc)]
    qs = [_stack_heads(q_ref, 0, slice(j * tq, (j + 1) * tq)) for j in range(nblk)]
    m = None
    s, v = scores(stages[0])
    for n, (j, c) in enumerate(stages):
        if n + 1 < len(stages):
            s_next, v_next = scores(stages[n + 1])
        acc = acc_scr.at[j]
        m_new = s.max(axis=0, keepdims=True)
        if c >= 0:
            m_new = jnp.maximum(m, m_new)
        pv = _dot(_values_with_ones(v), jnp.exp2(s - m_new).astype(BF16))
        if c < 0:
            acc[...] = pv
        else:
            acc[...] = jnp.exp2(m - m_new) * acc[...] + pv
        m = m_new
        if c == nc - 1:
            a = acc[...]
            o = a[:HEAD_DIM] / a[HEAD_DIM:HEAD_DIM + 1]
            for g in range(GROUP):
                o_ref[j * tq:(j + 1) * tq, g * HEAD_DIM:(g + 1) * HEAD_DIM] = (
                    o[:, g * tq:(g + 1) * tq].T.astype(BF16))
        if n + 1 < len(stages):
            s, v = s_next, v_next


def _attn_a(qa, kv, kv_ctx, batch, seq, ctx_len, ada_next=None, tq=TILES.attn_q, tk=TILES.attn_k,
            nblk=TILES.attn_a_blocks):
    nq = seq // (tq * nblk)
    gw = GROUP * HEAD_DIM
    in_specs = [
        pl.BlockSpec((tq * nblk, gw), lambda b, h, i: (b * nq + i, h)),
        pl.BlockSpec((ctx_len, HEAD_DIM), lambda b, h, i: (b, h)),
        pl.BlockSpec((ctx_len, HEAD_DIM), lambda b, h, i: (b, A_KV_HEADS + h)),
        pl.BlockSpec((seq, HEAD_DIM), lambda b, h, i: (b, h)),
        pl.BlockSpec((seq, HEAD_DIM), lambda b, h, i: (b, A_KV_HEADS + h)),
    ]
    out_shape = [jax.ShapeDtypeStruct(qa.shape, BF16)]
    out_specs = [pl.BlockSpec((tq * nblk, gw), lambda b, h, i: (b * nq + i, h))]
    args = [qa, kv_ctx, kv_ctx, kv, kv]
    if ada_next is not None:
        cond, w_ada, b_ada, layer = ada_next
        n = w_ada.shape[-1]
        nsteps = batch * A_KV_HEADS * nq
        slab = n // nsteps
        assert slab % HEAD_DIM == 0 and slab * nsteps == n
        step = lambda b, h, i: (b * A_KV_HEADS + h) * nq + i
        in_specs += [
            pl.BlockSpec((MOD_ROWS, D_MODEL), lambda b, h, i: (0, 0)),
            pl.BlockSpec((None, D_MODEL, slab), lambda b, h, i: (layer, 0, step(b, h, i))),
            pl.BlockSpec((None, 1, slab), lambda b, h, i: (layer, 0, step(b, h, i))),
        ]
        out_shape.append(jax.ShapeDtypeStruct((MOD_ROWS, n), F32))
        out_specs.append(pl.BlockSpec((MOD_ROWS, slab), lambda b, h, i: (0, step(b, h, i))))
        args += [cond, w_ada, b_ada.reshape(-1, 1, n)]
    outs = pl.pallas_call(
        functools.partial(_attn_a_kernel, tq, tk, nblk, ada_next is not None),
        out_shape=out_shape,
        grid=(batch, A_KV_HEADS, nq),
        in_specs=in_specs,
        out_specs=out_specs,
        scratch_shapes=[pltpu.VMEM((nblk, HEAD_DIM + BF16_SUBLANES, GROUP * tq), F32)],
        compiler_params=_params(("parallel", "parallel", "arbitrary")),
        name="attn_a",
    )(*args)
    return outs[0], (outs[1] if ada_next is not None else None)


def _attn_b_kernel(seq, nblk, sink_ref, q_ref, kvc_ref, kv_ref, o_ref):
    span = Q_BLOCK + 2 * WINDOW
    rows = GROUP * Q_BLOCK
    row_q = lax.broadcasted_iota(jnp.int32, (rows, span), 0) % Q_BLOCK
    col_k = lax.broadcasted_iota(jnp.int32, (rows, span), 1)
    work = []
    for j in range(nblk):
        blk = pl.program_id(1) * nblk + j
        qrows = slice(j * Q_BLOCK, (j + 1) * Q_BLOCK)
        start = pl.multiple_of(jnp.clip(blk * Q_BLOCK - WINDOW, 0, seq - span), Q_BLOCK)
        valid = jnp.abs(blk * Q_BLOCK - start + row_q - col_k) <= WINDOW
        for kvh in range(B_KV_HEADS):
            kc = slice(kvh * HEAD_DIM, (kvh + 1) * HEAD_DIM)
            q = _stack_heads(q_ref, kvh, qrows)
            s_win = jnp.where(valid, _dot_nt(q, kv_ref[pl.ds(start, span), kc]), NEG_INF)
            s_ctx = _dot_nt(q, kvc_ref[:, kc])
            work.append((qrows, kvh, start, s_win, s_ctx))
    for qrows, kvh, start, s_win, s_ctx in work:
        vc = slice((B_KV_HEADS + kvh) * HEAD_DIM, (B_KV_HEADS + kvh + 1) * HEAD_DIM)
        o = _softmax_pv([s_win, s_ctx], [kv_ref[pl.ds(start, span), vc], kvc_ref[:, vc]],
                        _sink_column(sink_ref, kvh, Q_BLOCK))
        _unstack_heads(o_ref, kvh, o, Q_BLOCK, qrows)


def _attn_b(qb, kv, kv_ctx, sink, batch, seq, ctx_len, nblk=TILES.attn_b_blocks):
    nq = seq // (Q_BLOCK * nblk)
    bw = 2 * B_KV_HEADS * HEAD_DIM
    return pl.pallas_call(
        functools.partial(_attn_b_kernel, seq, nblk),
        out_shape=jax.ShapeDtypeStruct(qb.shape, BF16),
        grid_spec=pltpu.PrefetchScalarGridSpec(
            num_scalar_prefetch=1,
            grid=(batch, nq),
            in_specs=[
                pl.BlockSpec((Q_BLOCK * nblk, Q_COLS), lambda b, i, s: (b * nq + i, 0)),
                pl.BlockSpec((ctx_len, bw), lambda b, i, s: (b, 1)),
                pl.BlockSpec((seq, bw), lambda b, i, s: (b, 1)),
            ],
            out_specs=pl.BlockSpec((Q_BLOCK * nblk, Q_COLS), lambda b, i, s: (b * nq + i, 0)),
        ),
        compiler_params=_params(("parallel", "arbitrary")),
        name="attn_b",
    )(sink, qb, kv_ctx, kv)


def _attn_ctx_kernel(ctx_len, sink_ref, qa_ref, qb_ref, kv_ref, oa_ref, ob_ref):
    for kvh in range(A_KV_HEADS):
        k = kv_ref[:, kvh * HEAD_DIM:(kvh + 1) * HEAD_DIM]
        v = kv_ref[:, (A_KV_HEADS + kvh) * HEAD_DIM:(A_KV_HEADS + kvh + 1) * HEAD_DIM]
        o = _softmax_pv([_dot_nt(_stack_heads(qa_ref, kvh), k)], [v])
        _unstack_heads(oa_ref, kvh, o, ctx_len)
    off = 2 * A_KV_HEADS
    for kvh in range(B_KV_HEADS):
        k = kv_ref[:, (off + kvh) * HEAD_DIM:(off + kvh + 1) * HEAD_DIM]
        v = kv_ref[:, (off + B_KV_HEADS + kvh) * HEAD_DIM:(off + B_KV_HEADS + kvh + 1) * HEAD_DIM]
        o = _softmax_pv([_dot_nt(_stack_heads(qb_ref, kvh), k)], [v], _sink_column(sink_ref, kvh, ctx_len))
        _unstack_heads(ob_ref, kvh, o, ctx_len)


def _attn_ctx(qa, qb, kv_ctx, sink, batch, ctx_len):
    q_spec = pl.BlockSpec((ctx_len, Q_COLS), lambda b, s: (b, 0))
    return pl.pallas_call(
        functools.partial(_attn_ctx_kernel, ctx_len),
        out_shape=[jax.ShapeDtypeStruct(qa.shape, BF16), jax.ShapeDtypeStruct(qb.shape, BF16)],
        grid_spec=pltpu.PrefetchScalarGridSpec(
            num_scalar_prefetch=1,
            grid=(batch,),
            in_specs=[q_spec, q_spec, pl.BlockSpec((ctx_len, KV_COLS), lambda b, s: (b, 0))],
            out_specs=[q_spec, q_spec],
        ),
        compiler_params=_params(("parallel",)),
        name="attn_ctx",
    )(sink, qa, qb, kv_ctx)


def _outproj_kernel(tm, blocks_per_seq, oa_ref, ob_ref, z_ref, zprev_ref, znext_ref, gp_ref, cw_ref, w_ref,
                    x_ref, gt_ref, gpost_ref, gffn_ref, sc_ref, sh_ref, xo_ref, h2_ref, oc_scr):
    i = pl.program_id(0)
    pos = i % blocks_per_seq
    z = z_ref[...].astype(F32)
    prev_row = jnp.where(pos == 0, 0.0, zprev_ref[BF16_SUBLANES - 1:BF16_SUBLANES, :].astype(F32))
    next_row = jnp.where(pos == blocks_per_seq - 1, 0.0, znext_ref[0:1, :].astype(F32))
    row = lax.broadcasted_iota(jnp.int32, z.shape, 0)
    z_prev = jnp.where(row == 0, prev_row, pltpu.roll(z, 1, 0))
    z_next = jnp.where(row == tm - 1, next_row, pltpu.roll(z, tm - 1, 0))
    cw = cw_ref[...]
    oc = gp_ref[...].astype(F32) * (z_prev * cw[0:1] + z * cw[1:2] + z_next * cw[2:3])
    oc_scr[...] = oc.astype(BF16)

    post_gain = gt_ref[...] * gpost_ref[...]
    ffn_gain = gffn_ref[...] * (1.0 + sc_ref[...])
    shift = sh_ref[...]
    sub = min(tm, TILES.row_sub)
    for r in range(tm // sub):
        rows = slice(r * sub, (r + 1) * sub)
        y = _dot(oa_ref[rows, :], w_ref[0:Q_COLS, :])
        y += _dot(ob_ref[rows, :], w_ref[Q_COLS:2 * Q_COLS, :])
        y += _dot(oc_scr[rows, :], w_ref[2 * Q_COLS:, :])
        x = x_ref[rows, :] + _rms(y) * post_gain
        xo_ref[rows, :] = x
        h2_ref[rows, :] = (_rms(x) * ffn_gain + shift).astype(BF16)


def _outproj(oa, ob, z, gp, conv_w, w_out, x, mod, row_of_block, g_post, g_ffn, tm, seq_len):
    m = x.shape[0]
    blocks_per_seq = seq_len // tm
    halo = tm // BF16_SUBLANES
    last_halo = m // BF16_SUBLANES - 1
    row_spec = lambda w: pl.BlockSpec((tm, w), lambda i: (i, 0))
    return pl.pallas_call(
        functools.partial(_outproj_kernel, tm, blocks_per_seq),
        out_shape=[jax.ShapeDtypeStruct((m, D_MODEL), F32), jax.ShapeDtypeStruct((m, D_MODEL), BF16)],
        grid=(m // tm,),
        in_specs=[
            row_spec(Q_COLS),
            row_spec(Q_COLS),
            row_spec(CONV_CH),
            pl.BlockSpec((BF16_SUBLANES, CONV_CH), lambda i: (jnp.maximum(i * halo - 1, 0), 0)),
            pl.BlockSpec((BF16_SUBLANES, CONV_CH), lambda i: (jnp.minimum((i + 1) * halo, last_halo), 0)),
            row_spec(CONV_CH),
            pl.BlockSpec(conv_w.shape, lambda i: (0, 0)),
            _resident((D_MODEL, D_MODEL), lambda i: (0, 0)),
            row_spec(D_MODEL),
            _mod_spec(2, row_of_block),
            _vec_spec(D_MODEL),
            _vec_spec(D_MODEL),
            _mod_spec(4, row_of_block),
            _mod_spec(3, row_of_block),
        ],
        out_specs=[row_spec(D_MODEL), row_spec(D_MODEL)],
        scratch_shapes=[pltpu.VMEM((tm, CONV_CH), BF16)],
        compiler_params=_params(("parallel",)),
        name="outproj",
    )(oa, ob, z, z, z, gp, conv_w, w_out, x, mod, g_post, g_ffn, mod, mod)


def _ffn_up_kernel(tm, round_down, h_ref, wg_ref, wu_ref, *rest):
    if round_down:
        wd_ref, a_ref, wd16_ref = rest
        wd16_ref[...] = wd_ref[...].astype(BF16)
    else:
        a_ref, = rest
    wg = wg_ref[...].astype(BF16)
    wu = wu_ref[...].astype(BF16)
    sub = min(tm, TILES.row_sub_up)
    for r in range(tm // sub):
        rows = slice(r * sub, (r + 1) * sub)
        h = h_ref[rows, :]
        a_ref[rows, :] = (_silu(_dot(h, wg)) * _dot(h, wu)).astype(BF16)


def _ffn_down_kernel(tm, round_next, a_ref, wd_ref, x_ref, gt_ref, g_ref, *rest):
    if round_next:
        win_ref, wout_ref, o_ref, win16_ref, wout16_ref = rest
        win16_ref[...] = win_ref[...].astype(BF16)
        wout16_ref[...] = wout_ref[...].astype(BF16)
    else:
        o_ref, = rest
    gain = gt_ref[...] * g_ref[...]
    sub = min(tm, TILES.row_sub)
    for r in range(tm // sub):
        rows = slice(r * sub, (r + 1) * sub)
        o_ref[rows, :] = x_ref[rows, :] + _rms(_dot(a_ref[rows, :], wd_ref[...])) * gain


def _ffn(h2, w_gate, w_up, w_down, layer, x, mod, row_of_block, g_post, tm_up, tm_down, w_down16=None,
         next_proj=None, tf=TILES.ffn_cols):
    m = x.shape[0]
    hidden = w_gate.shape[-1]
    nf = hidden // tf
    round_down = w_down16 is None
    in_specs = [
        pl.BlockSpec((tm_up, D_MODEL), lambda i, f: (i, 0)),
        pl.BlockSpec((None, D_MODEL, tf), lambda i, f: (layer, 0, f)),
        pl.BlockSpec((None, D_MODEL, tf), lambda i, f: (layer, 0, f)),
    ]
    out_shape = [jax.ShapeDtypeStruct((m, hidden), BF16)]
    out_specs = [pl.BlockSpec((tm_up, tf), lambda i, f: (i, f))]
    args = [h2, w_gate, w_up]
    if round_down:
        slab = hidden // ((m // tm_up) * nf)
        assert slab % BF16_SUBLANES == 0 and slab * (m // tm_up) * nf == hidden
        in_specs.append(pl.BlockSpec((None, slab, D_MODEL), lambda i, f: (layer, i * nf + f, 0)))
        out_shape.append(jax.ShapeDtypeStruct((hidden, D_MODEL), BF16))
        out_specs.append(pl.BlockSpec((slab, D_MODEL), lambda i, f: (i * nf + f, 0)))
        args.append(w_down)
    outs = pl.pallas_call(
        functools.partial(_ffn_up_kernel, tm_up, round_down),
        out_shape=out_shape,
        grid=(m // tm_up, nf),
        in_specs=in_specs,
        out_specs=out_specs,
        compiler_params=_params(("parallel", "arbitrary")),
        name="ffn_up",
    )(*args)
    act = outs[0]
    if round_down:
        w_down16 = outs[1]
    in_specs = [
        pl.BlockSpec((tm_down, hidden), lambda i: (i, 0)),
        _resident((hidden, D_MODEL), lambda i: (0, 0)),
        pl.BlockSpec((tm_down, D_MODEL), lambda i: (i, 0)),
        _mod_spec(5, row_of_block),
        _vec_spec(D_MODEL),
    ]
    out_shape = [jax.ShapeDtypeStruct((m, D_MODEL), F32)]
    out_specs = [pl.BlockSpec((tm_down, D_MODEL), lambda i: (i, 0))]
    args = [act, w_down16, x, mod, g_post]
    if next_proj is not None:
        slab = D_MODEL // (m // tm_down)
        assert slab % BF16_SUBLANES == 0 and slab * (m // tm_down) == D_MODEL
        for w in next_proj:
            cols = w.shape[-1]
            in_specs.append(pl.BlockSpec((None, slab, cols), lambda i: (layer + 1, i, 0)))
            out_shape.append(jax.ShapeDtypeStruct((D_MODEL, cols), BF16))
            out_specs.append(pl.BlockSpec((slab, cols), lambda i: (i, 0)))
            args.append(w)
    outs = pl.pallas_call(
        functools.partial(_ffn_down_kernel, tm_down, next_proj is not None),
        out_shape=out_shape,
        grid=(m // tm_down,),
        in_specs=in_specs,
        out_specs=out_specs,
        compiler_params=_params(("parallel",)),
        name="ffn_down",
    )(*args)
    return outs[0], w_down16, tuple(outs[1:])


def _rope_tables(seq):
    rows = seq // GRID_W
    inv_freq = jnp.power(ROPE_BASE, -jnp.arange(ROPE_PAIRS, dtype=F32) / ROPE_PAIRS)
    ang_row = jnp.arange(rows, dtype=jnp.int32).astype(F32)[:, None] * inv_freq
    ang_col = jnp.arange(GRID_W, dtype=jnp.int32).astype(F32)[:, None] * inv_freq
    by_row = lambda t: jnp.repeat(t, GRID_W, axis=0)
    by_col = lambda t: jnp.tile(t, (rows, 1))
    cos_r, sin_r = by_row(jnp.cos(ang_row)), by_row(jnp.sin(ang_row))
    cos_c, sin_c = by_col(jnp.cos(ang_col)), by_col(jnp.sin(ang_col))
    cos_t = jnp.concatenate([cos_r, cos_r, cos_c, cos_c], axis=-1)
    sin_t = jnp.concatenate([-sin_r, sin_r, -sin_c, sin_c], axis=-1)
    return cos_t, sin_t


def kernel(x, c, ctx, c_ctx, w_ada, b_ada, g_mix_pre, g_mix_post, g_ffn_pre, g_ffn_post, w_in, q_norm, k_norm,
           sink, conv_w, w_out, w_gate, w_up, w_down):
    batch, seq, d = x.shape
    ctx_len = ctx.shape[1]
    depth = w_ada.shape[0]
    assert d == D_MODEL and batch < MOD_ROWS and ctx_len % Q_BLOCK == 0
    assert seq % TILES.row_in == 0 and (batch * seq) % TILES.row_up == 0 and seq % GRID_W == 0

    tm, tm_in, tm_up = TILES.row, TILES.row_in, TILES.row_up
    tm_ctx = ctx_len
    tm_ctx_ffn = min(tm, batch * ctx_len)
    lat_row = lambda i: i // (seq // tm)
    lat_row_in = lambda i: i // (seq // tm_in)
    ctx_row = lambda i: batch

    cond = jnp.zeros((MOD_ROWS, d), F32).at[:batch].set(c).at[batch].set(c_ctx)
    as_table = lambda t: t.reshape(MOD_ROWS * N_MOD, 1, d)
    mod = as_table(_ada(cond, w_ada, b_ada, layers=1)[0])
    tables = _rope_tables(seq)
    w_in16, w_out16 = w_in[0].astype(BF16), w_out[0].astype(BF16)

    xl = x.reshape(batch * seq, d)
    xc = ctx.reshape(batch * ctx_len, d)
    vec = lambda a, l: a[l].reshape(1, -1)
    for l in range(depth):
        g_pre, g_post = vec(g_mix_pre, l), vec(g_mix_post, l)
        g_ffn, g_ffn_out = vec(g_ffn_pre, l), vec(g_ffn_post, l)
        qn, kn = vec(q_norm, l), vec(k_norm, l)
        update_ctx = l < depth - 1

        ctx_proj = _inproj(xc, mod, ctx_row, g_pre, w_in16, qn, kn, None, tm_ctx, ctx_len, kv_only=not update_ctx)
        kv_c = ctx_proj[0]
        kv, qa, qb, z, gp = _inproj(xl, mod, lat_row_in, g_pre, w_in16, qn, kn, tables, tm_in, seq)
        oa, mod_next = _attn_a(qa, kv, kv_c, batch, seq, ctx_len,
                               ada_next=(cond, w_ada, b_ada, l + 1) if l + 1 < depth else None)
        ob = _attn_b(qb, kv, kv_c, sink[l], batch, seq, ctx_len)
        xl, h2 = _outproj(oa, ob, z, gp, conv_w[l], w_out16, xl, mod, lat_row, g_post, g_ffn, tm, seq)
        xl, w_down16, next16 = _ffn(h2, w_gate, w_up, w_down, l, xl, mod, lat_row, g_ffn_out, tm_up, tm,
                                    next_proj=(w_in, w_out) if l + 1 < depth else None)
        if update_ctx:
            _, qa_c, qb_c, z_c, gp_c = ctx_proj
            oa_c, ob_c = _attn_ctx(qa_c, qb_c, kv_c, sink[l], batch, ctx_len)
            xc, h2_c = _outproj(oa_c, ob_c, z_c, gp_c, conv_w[l], w_out16, xc, mod, ctx_row, g_post, g_ffn,
                                tm_ctx, ctx_len)
            xc, _, _ = _ffn(h2_c, w_gate, w_up, w_down, l, xc, mod, ctx_row, g_ffn_out,
                            min(tm_up, batch * ctx_len), tm_ctx_ffn, w_down16)
        if l + 1 < depth:
            w_in16, w_out16 = next16
            mod = as_table(mod_next)
    return xl.reshape(batch, seq, d)
```

```python
import functools
from typing import NamedTuple

import jax
import jax.numpy as jnp
from jax import lax
from jax.experimental import pallas as pl
from jax.experimental.pallas import tpu as pltpu

F32 = jnp.float32
BF16 = jnp.bfloat16

D_MODEL = 2048
HEAD_DIM = 128
GRID_W = 64
A_Q_HEADS = 6
A_KV_HEADS = 2
B_Q_HEADS = 6
B_KV_HEADS = 2
GROUP = A_Q_HEADS // A_KV_HEADS
Q_COLS = A_Q_HEADS * HEAD_DIM
CONV_CH = D_MODEL - 2 * Q_COLS
WINDOW = 128
Q_BLOCK = 128
ROPE_BASE = 10000.0
ROPE_PAIRS = HEAD_DIM // 4
NORM_EPS = 1e-6
NEG_INF = -1e30
ATTN_SCALE = HEAD_DIM ** -0.5
LOG2_E = 1.4426950408889634
SCORE_SCALE = ATTN_SCALE * LOG2_E
N_MOD = 6
MOD_ROWS = 8

KV_COLS = 4 * A_KV_HEADS * HEAD_DIM
OFF_AQ = KV_COLS
OFF_BQ = OFF_AQ + Q_COLS
OFF_CONV = OFF_BQ + Q_COLS
IN_COLS = OFF_CONV + 3 * CONV_CH

V7X_VMEM_BYTES = 64 * 1024 * 1024
VMEM_LIMIT = V7X_VMEM_BYTES - 8 * 1024 * 1024
BF16_SUBLANES = 16


class _Tiles(NamedTuple):
    row: int = 512
    row_in: int = 1024
    row_up: int = 2048
    row_sub: int = 256
    row_sub_up: int = 128
    ffn_cols: int = 512
    ada_cols: int = 1024
    attn_q: int = 256
    attn_k: int = 512
    attn_a_blocks: int = 4
    attn_b_blocks: int = 4


TILES = _Tiles()


def _rms(t):
    return t * lax.rsqrt(jnp.mean(t * t, axis=-1, keepdims=True) + NORM_EPS)


def _silu(t):
    return t * (1.0 / (1.0 + jnp.exp(-t)))


def _dot(a, b):
    return jnp.dot(a, b, preferred_element_type=F32)


def _dot_nt(a, b):
    return lax.dot_general(a, b, (((1,), (1,)), ((), ())), preferred_element_type=F32)


def _params(semantics):
    return pltpu.CompilerParams(dimension_semantics=semantics, vmem_limit_bytes=VMEM_LIMIT)


def _resident(shape, index_map):
    return pl.BlockSpec(shape, index_map, pipeline_mode=pl.Buffered(1))


def _ada_slab(s_ref, w_ref, b_ref):
    return _dot(_silu(s_ref[...]).astype(BF16), w_ref[...].astype(BF16)) + b_ref[...]


def _ada_kernel(s_ref, w_ref, b_ref, o_ref):
    o_ref[0] = _ada_slab(s_ref, w_ref.at[0], b_ref.at[0])


def _ada(cond, w_ada, b_ada, layers, tn=TILES.ada_cols):
    _, d, n = w_ada.shape
    return pl.pallas_call(
        _ada_kernel,
        out_shape=jax.ShapeDtypeStruct((layers, MOD_ROWS, n), F32),
        grid=(layers, n // tn),
        in_specs=[
            pl.BlockSpec((MOD_ROWS, d), lambda l, j: (0, 0)),
            pl.BlockSpec((1, d, tn), lambda l, j: (l, 0, j)),
            pl.BlockSpec((1, 1, tn), lambda l, j: (l, 0, j)),
        ],
        out_specs=pl.BlockSpec((1, MOD_ROWS, tn), lambda l, j: (l, 0, j)),
        compiler_params=_params(("parallel", "parallel")),
        name="ada",
    )(cond, w_ada, b_ada.reshape(-1, 1, n))


def _mod_spec(chunk, row_of_block):
    return pl.BlockSpec((None, 1, D_MODEL), lambda i: (row_of_block(i) * N_MOD + chunk, 0, 0))


def _vec_spec(n):
    return pl.BlockSpec((1, n), lambda i: (0, 0))


def _inproj_kernel(rope, kv_only, round_out, tm, x_ref, g_ref, sc_ref, sh_ref, w_ref, qn_ref, kn_ref, *rest):
    if rope:
        cos_ref, sin_ref = rest[:2]
        rest = rest[2:]
    if round_out:
        rest[-1][...] = rest[0][...].astype(BF16)
        rest = rest[1:-1]
    if kv_only:
        kv_ref, = rest
    else:
        kv_ref, qa_ref, qb_ref, z_ref, gp_ref = rest
    sub = min(tm, TILES.row_sub)
    pre_gain = g_ref[...] * (1.0 + sc_ref[...])
    shift = sh_ref[...]
    k_gain = kn_ref[...] * SCORE_SCALE
    q_gain = qn_ref[...]
    lane = lax.broadcasted_iota(jnp.int32, (sub, HEAD_DIM), 1)
    first_half = (lane & ROPE_PAIRS) == 0

    for r in range(tm // sub):
        rows = slice(r * sub, (r + 1) * sub)
        if rope:
            cos = cos_ref[rows, :]
            sin = sin_ref[rows, :]

            def rot(t):
                partner = jnp.where(first_half, pltpu.roll(t, HEAD_DIM - ROPE_PAIRS, 1),
                                    pltpu.roll(t, ROPE_PAIRS, 1))
                return t * cos + partner * sin
        else:
            def rot(t):
                return t

        hb = (_rms(x_ref[rows, :]) * pre_gain + shift).astype(BF16)

        def proj(c0, n):
            return _dot(hb, w_ref[:, c0:c0 + n])

        def head(t, k):
            return t[:, k * HEAD_DIM:(k + 1) * HEAD_DIM]

        def put(ref, k, t):
            ref[rows, k * HEAD_DIM:(k + 1) * HEAD_DIM] = t.astype(BF16)

        pkv = proj(0, KV_COLS)
        for k in range(A_KV_HEADS):
            put(kv_ref, k, rot(_rms(head(pkv, k)) * k_gain))
            put(kv_ref, A_KV_HEADS + k, head(pkv, A_KV_HEADS + k))
            put(kv_ref, 2 * A_KV_HEADS + k, rot(head(pkv, 2 * A_KV_HEADS + k)) * SCORE_SCALE)
            put(kv_ref, 3 * A_KV_HEADS + k, head(pkv, 3 * A_KV_HEADS + k))
        if kv_only:
            continue

        pq = proj(OFF_AQ, Q_COLS)
        for k in range(A_Q_HEADS):
            put(qa_ref, k, rot(_rms(head(pq, k)) * q_gain))
        pq = proj(OFF_BQ, Q_COLS)
        for k in range(B_Q_HEADS):
            put(qb_ref, k, rot(head(pq, k)))

        z_ref[rows, :] = (proj(OFF_CONV, CONV_CH) * proj(OFF_CONV + 2 * CONV_CH, CONV_CH)).astype(BF16)
        gp_ref[rows, :] = proj(OFF_CONV + CONV_CH, CONV_CH).astype(BF16)


def _inproj(x, mod, row_of_block, g_pre, w_in, q_norm, k_norm, rope_tables, tm, seq_len, kv_only=False,
            round_out=None):
    m = x.shape[0]
    rope = rope_tables is not None
    cols = KV_COLS if kv_only else IN_COLS
    in_specs = [
        pl.BlockSpec((tm, D_MODEL), lambda i: (i, 0)),
        _vec_spec(D_MODEL),
        _mod_spec(1, row_of_block),
        _mod_spec(0, row_of_block),
        _resident((D_MODEL, cols), lambda i: (0, 0)),
        _vec_spec(HEAD_DIM),
        _vec_spec(HEAD_DIM),
    ]
    args = [x, g_pre, mod, mod, w_in, q_norm, k_norm]
    if rope:
        blocks_per_seq = seq_len // tm
        in_specs += [pl.BlockSpec((tm, HEAD_DIM), lambda i: (i % blocks_per_seq, 0))] * 2
        args += list(rope_tables)
    widths = (KV_COLS,) if kv_only else (KV_COLS, Q_COLS, Q_COLS, CONV_CH, CONV_CH)
    out_shape = [jax.ShapeDtypeStruct((m, w), BF16) for w in widths]
    out_specs = [pl.BlockSpec((tm, w), lambda i: (i, 0)) for w in widths]
    if round_out is not None:
        w_out, layer = round_out
        slab = D_MODEL // (m // tm)
        assert slab % BF16_SUBLANES == 0 and slab * (m // tm) == D_MODEL
        in_specs.append(pl.BlockSpec((None, slab, D_MODEL), lambda i: (layer, i, 0)))
        args.append(w_out)
        out_shape.append(jax.ShapeDtypeStruct((D_MODEL, D_MODEL), BF16))
        out_specs.append(pl.BlockSpec((slab, D_MODEL), lambda i: (i, 0)))
    return pl.pallas_call(
        functools.partial(_inproj_kernel, rope, kv_only, round_out is not None, tm),
        out_shape=out_shape,
        grid=(m // tm,),
        in_specs=in_specs,
        out_specs=out_specs,
        compiler_params=_params(("parallel",)),
        name=("inproj_rope" if rope else "inproj") + ("_kv" if kv_only else ""),
    )(*args)


def _stack_heads(q_ref, kvh, rows=slice(None)):
    base = kvh * GROUP * HEAD_DIM
    return jnp.concatenate(
        [q_ref[rows, base + g * HEAD_DIM: base + (g + 1) * HEAD_DIM] for g in range(GROUP)], axis=0)


def _unstack_heads(o_ref, kvh, o, tq, rows=slice(None)):
    base = kvh * GROUP * HEAD_DIM
    for g in range(GROUP):
        o_ref[rows, base + g * HEAD_DIM: base + (g + 1) * HEAD_DIM] = o[g * tq:(g + 1) * tq].astype(BF16)


def _sink_column(sink_ref, kvh, tq):
    grp = lax.broadcasted_iota(jnp.int32, (GROUP * tq, 1), 0) // tq
    col = jnp.full((GROUP * tq, 1), sink_ref[kvh * GROUP], F32)
    for g in range(1, GROUP):
        col = jnp.where(grp == g, sink_ref[kvh * GROUP + g], col)
    return col * LOG2_E


def _softmax_pv(scores, values, sink_col=None):
    tiles = [s[:, t * HEAD_DIM:(t + 1) * HEAD_DIM] for s in scores for t in range(s.shape[1] // HEAD_DIM)]
    m = functools.reduce(jnp.maximum, tiles).max(axis=-1, keepdims=True)
    if sink_col is not None:
        m = jnp.maximum(m, sink_col)
    acc = None
    for s, v in zip(scores, values):
        v1 = jnp.concatenate([v, jnp.ones_like(v)], axis=1)
        a = _dot(jnp.exp2(s - m).astype(BF16), v1)
        acc = a if acc is None else acc + a
    den = acc[:, HEAD_DIM:]
    if sink_col is not None:
        den = den + jnp.exp2(sink_col - m)
    return acc[:, :HEAD_DIM] / den


def _values_with_ones(v):
    return jnp.concatenate([v.T, jnp.ones((BF16_SUBLANES, v.shape[0]), BF16)], axis=0)


def _attn_a_kernel(tq, tk, nblk, ada_next, q_ref, kc_ref, vc_ref, k_ref, v_ref, *rest):
    if ada_next:
        cond_ref, wa_ref, ba_ref, o_ref, mod_ref, acc_scr = rest
        mod_ref[...] = _ada_slab(cond_ref, wa_ref, ba_ref)
    else:
        o_ref, acc_scr = rest
    nc = k_ref.shape[0] // tk

    def scores(stage):
        j, c = stage
        k, v = (kc_ref[...], vc_ref[...]) if c < 0 else (k_ref[c * tk:(c + 1) * tk, :], v_ref[c * tk:(c + 1) * tk, :])
        return _dot_nt(k, qs[j]), v

    stages = [(j, c) for j in range(nblk) for c in range(-1, nc)]
    qs = [_stack_heads(q_ref, 0, slice(j * tq, (j + 1) * tq)) for j in range(nblk)]
    m = None
    s, v = scores(stages[0])
    for n, (j, c) in enumerate(stages):
        if n + 1 < len(stages):
            s_next, v_next = scores(stages[n + 1])
        acc = acc_scr.at[j]
        m_new = s.max(axis=0, keepdims=True)
        if c >= 0:
            m_new = jnp.maximum(m, m_new)
        pv = _dot(_values_with_ones(v), jnp.exp2(s - m_new).astype(BF16))
        if c < 0:
            acc[...] = pv
        else:
            acc[...] = jnp.exp2(m - m_new) * acc[...] + pv
        m = m_new
        if c == nc - 1:
            a = acc[...]
            o = a[:HEAD_DIM] / a[HEAD_DIM:HEAD_DIM + 1]
            for g in range(GROUP):
                o_ref[j * tq:(j + 1) * tq, g * HEAD_DIM:(g + 1) * HEAD_DIM] = (
                    o[:, g * tq:(g + 1) * tq].T.astype(BF16))
        if n + 1 < len(stages):
            s, v = s_next, v_next


def _attn_a(qa, kv, kv_ctx, batch, seq, ctx_len, ada_next=None, tq=TILES.attn_q, tk=TILES.attn_k,
            nblk=TILES.attn_a_blocks):
    nq = seq // (tq * nblk)
    gw = GROUP * HEAD_DIM
    in_specs = [
        pl.BlockSpec((tq * nblk, gw), lambda b, h, i: (b * nq + i, h)),
        pl.BlockSpec((ctx_len, HEAD_DIM), lambda b, h, i: (b, h)),
        pl.BlockSpec((ctx_len, HEAD_DIM), lambda b, h, i: (b, A_KV_HEADS + h)),
        pl.BlockSpec((seq, HEAD_DIM), lambda b, h, i: (b, h)),
        pl.BlockSpec((seq, HEAD_DIM), lambda b, h, i: (b, A_KV_HEADS + h)),
    ]
    out_shape = [jax.ShapeDtypeStruct(qa.shape, BF16)]
    out_specs = [pl.BlockSpec((tq * nblk, gw), lambda b, h, i: (b * nq + i, h))]
    args = [qa, kv_ctx, kv_ctx, kv, kv]
    if ada_next is not None:
        cond, w_ada, b_ada, layer = ada_next
        n = w_ada.shape[-1]
        nsteps = batch * A_KV_HEADS * nq
        slab = n // nsteps
        assert slab % HEAD_DIM == 0 and slab * nsteps == n
        step = lambda b, h, i: (b * A_KV_HEADS + h) * nq + i
        in_specs += [
            pl.BlockSpec((MOD_ROWS, D_MODEL), lambda b, h, i: (0, 0)),
            pl.BlockSpec((None, D_MODEL, slab), lambda b, h, i: (layer, 0, step(b, h, i))),
            pl.BlockSpec((None, 1, slab), lambda b, h, i: (layer, 0, step(b, h, i))),
        ]
        out_shape.append(jax.ShapeDtypeStruct((MOD_ROWS, n), F32))
        out_specs.append(pl.BlockSpec((MOD_ROWS, slab), lambda b, h, i: (0, step(b, h, i))))
        args += [cond, w_ada, b_ada.reshape(-1, 1, n)]
    outs = pl.pallas_call(
        functools.partial(_attn_a_kernel, tq, tk, nblk, ada_next is not None),
        out_shape=out_shape,
        grid=(batch, A_KV_HEADS, nq),
        in_specs=in_specs,
        out_specs=out_specs,
        scratch_shapes=[pltpu.VMEM((nblk, HEAD_DIM + BF16_SUBLANES, GROUP * tq), F32)],
        compiler_params=_params(("parallel", "parallel", "arbitrary")),
        name="attn_a",
    )(*args)
    return outs[0], (outs[1] if ada_next is not None else None)


def _attn_b_kernel(seq, nblk, sink_ref, q_ref, kvc_ref, kv_ref, o_ref):
    span = Q_BLOCK + 2 * WINDOW
    rows = GROUP * Q_BLOCK
    row_q = lax.broadcasted_iota(jnp.int32, (rows, span), 0) % Q_BLOCK
    col_k = lax.broadcasted_iota(jnp.int32, (rows, span), 1)
    work = []
    for j in range(nblk):
        blk = pl.program_id(1) * nblk + j
        qrows = slice(j * Q_BLOCK, (j + 1) * Q_BLOCK)
        start = pl.multiple_of(jnp.clip(blk * Q_BLOCK - WINDOW, 0, seq - span), Q_BLOCK)
        valid = jnp.abs(blk * Q_BLOCK - start + row_q - col_k) <= WINDOW
        for kvh in range(B_KV_HEADS):
            kc = slice(kvh * HEAD_DIM, (kvh + 1) * HEAD_DIM)
            q = _stack_heads(q_ref, kvh, qrows)
            s_win = jnp.where(valid, _dot_nt(q, kv_ref[pl.ds(start, span), kc]), NEG_INF)
            s_ctx = _dot_nt(q, kvc_ref[:, kc])
            work.append((qrows, kvh, start, s_win, s_ctx))
    for qrows, kvh, start, s_win, s_ctx in work:
        vc = slice((B_KV_HEADS + kvh) * HEAD_DIM, (B_KV_HEADS + kvh + 1) * HEAD_DIM)
        o = _softmax_pv([s_win, s_ctx], [kv_ref[pl.ds(start, span), vc], kvc_ref[:, vc]],
                        _sink_column(sink_ref, kvh, Q_BLOCK))
        _unstack_heads(o_ref, kvh, o, Q_BLOCK, qrows)


def _attn_b(qb, kv, kv_ctx, sink, batch, seq, ctx_len, nblk=TILES.attn_b_blocks):
    nq = seq // (Q_BLOCK * nblk)
    bw = 2 * B_KV_HEADS * HEAD_DIM
    return pl.pallas_call(
        functools.partial(_attn_b_kernel, seq, nblk),
        out_shape=jax.ShapeDtypeStruct(qb.shape, BF16),
        grid_spec=pltpu.PrefetchScalarGridSpec(
            num_scalar_prefetch=1,
            grid=(batch, nq),
            in_specs=[
                pl.BlockSpec((Q_BLOCK * nblk, Q_COLS), lambda b, i, s: (b * nq + i, 0)),
                pl.BlockSpec((ctx_len, bw), lambda b, i, s: (b, 1)),
                pl.BlockSpec((seq, bw), lambda b, i, s: (b, 1)),
            ],
            out_specs=pl.BlockSpec((Q_BLOCK * nblk, Q_COLS), lambda b, i, s: (b * nq + i, 0)),
        ),
        compiler_params=_params(("parallel", "arbitrary")),
        name="attn_b",
    )(sink, qb, kv_ctx, kv)


def _attn_ctx_kernel(ctx_len, sink_ref, qa_ref, qb_ref, kv_ref, oa_ref, ob_ref):
    for kvh in range(A_KV_HEADS):
        k = kv_ref[:, kvh * HEAD_DIM:(kvh + 1) * HEAD_DIM]
        v = kv_ref[:, (A_KV_HEADS + kvh) * HEAD_DIM:(A_KV_HEADS + kvh + 1) * HEAD_DIM]
        o = _softmax_pv([_dot_nt(_stack_heads(qa_ref, kvh), k)], [v])
        _unstack_heads(oa_ref, kvh, o, ctx_len)
    off = 2 * A_KV_HEADS
    for kvh in range(B_KV_HEADS):
        k = kv_ref[:, (off + kvh) * HEAD_DIM:(off + kvh + 1) * HEAD_DIM]
        v = kv_ref[:, (off + B_KV_HEADS + kvh) * HEAD_DIM:(off + B_KV_HEADS + kvh + 1) * HEAD_DIM]
        o = _softmax_pv([_dot_nt(_stack_heads(qb_ref, kvh), k)], [v], _sink_column(sink_ref, kvh, ctx_len))
        _unstack_heads(ob_ref, kvh, o, ctx_len)


def _attn_ctx(qa, qb, kv_ctx, sink, batch, ctx_len):
    q_spec = pl.BlockSpec((ctx_len, Q_COLS), lambda b, s: (b, 0))
    return pl.pallas_call(
        functools.partial(_attn_ctx_kernel, ctx_len),
        out_shape=[jax.ShapeDtypeStruct(qa.shape, BF16), jax.ShapeDtypeStruct(qb.shape, BF16)],
        grid_spec=pltpu.PrefetchScalarGridSpec(
            num_scalar_prefetch=1,
            grid=(batch,),
            in_specs=[q_spec, q_spec, pl.BlockSpec((ctx_len, KV_COLS), lambda b, s: (b, 0))],
            out_specs=[q_spec, q_spec],
        ),
        compiler_params=_params(("parallel",)),
        name="attn_ctx",
    )(sink, qa, qb, kv_ctx)


def _outproj_kernel(tm, blocks_per_seq, oa_ref, ob_ref, z_ref, zprev_ref, znext_ref, gp_ref, cw_ref, w_ref,
                    x_ref, gt_ref, gpost_ref, gffn_ref, sc_ref, sh_ref, xo_ref, h2_ref, oc_scr):
    i = pl.program_id(0)
    pos = i % blocks_per_seq
    z = z_ref[...].astype(F32)
    prev_row = jnp.where(pos == 0, 0.0, zprev_ref[BF16_SUBLANES - 1:BF16_SUBLANES, :].astype(F32))
    next_row = jnp.where(pos == blocks_per_seq - 1, 0.0, znext_ref[0:1, :].astype(F32))
    row = lax.broadcasted_iota(jnp.int32, z.shape, 0)
    z_prev = jnp.where(row == 0, prev_row, pltpu.roll(z, 1, 0))
    z_next = jnp.where(row == tm - 1, next_row, pltpu.roll(z, tm - 1, 0))
    cw = cw_ref[...]
    oc = gp_ref[...].astype(F32) * (z_prev * cw[0:1] + z * cw[1:2] + z_next * cw[2:3])
    oc_scr[...] = oc.astype(BF16)

    post_gain = gt_ref[...] * gpost_ref[...]
    ffn_gain = gffn_ref[...] * (1.0 + sc_ref[...])
    shift = sh_ref[...]
    sub = min(tm, TILES.row_sub)
    for r in range(tm // sub):
        rows = slice(r * sub, (r + 1) * sub)
        y = _dot(oa_ref[rows, :], w_ref[0:Q_COLS, :])
        y += _dot(ob_ref[rows, :], w_ref[Q_COLS:2 * Q_COLS, :])
        y += _dot(oc_scr[rows, :], w_ref[2 * Q_COLS:, :])
        x = x_ref[rows, :] + _rms(y) * post_gain
        xo_ref[rows, :] = x
        h2_ref[rows, :] = (_rms(x) * ffn_gain + shift).astype(BF16)


def _outproj(oa, ob, z, gp, conv_w, w_out, x, mod, row_of_block, g_post, g_ffn, tm, seq_len):
    m = x.shape[0]
    blocks_per_seq = seq_len // tm
    halo = tm // BF16_SUBLANES
    last_halo = m // BF16_SUBLANES - 1
    row_spec = lambda w: pl.BlockSpec((tm, w), lambda i: (i, 0))
    return pl.pallas_call(
        functools.partial(_outproj_kernel, tm, blocks_per_seq),
        out_shape=[jax.ShapeDtypeStruct((m, D_MODEL), F32), jax.ShapeDtypeStruct((m, D_MODEL), BF16)],
        grid=(m // tm,),
        in_specs=[
            row_spec(Q_COLS),
            row_spec(Q_COLS),
            row_spec(CONV_CH),
            pl.BlockSpec((BF16_SUBLANES, CONV_CH), lambda i: (jnp.maximum(i * halo - 1, 0), 0)),
            pl.BlockSpec((BF16_SUBLANES, CONV_CH), lambda i: (jnp.minimum((i + 1) * halo, last_halo), 0)),
            row_spec(CONV_CH),
            pl.BlockSpec(conv_w.shape, lambda i: (0, 0)),
            _resident((D_MODEL, D_MODEL), lambda i: (0, 0)),
            row_spec(D_MODEL),
            _mod_spec(2, row_of_block),
            _vec_spec(D_MODEL),
            _vec_spec(D_MODEL),
            _mod_spec(4, row_of_block),
            _mod_spec(3, row_of_block),
        ],
        out_specs=[row_spec(D_MODEL), row_spec(D_MODEL)],
        scratch_shapes=[pltpu.VMEM((tm, CONV_CH), BF16)],
        compiler_params=_params(("parallel",)),
        name="outproj",
    )(oa, ob, z, z, z, gp, conv_w, w_out, x, mod, g_post, g_ffn, mod, mod)


def _ffn_up_kernel(tm, round_down, h_ref, wg_ref, wu_ref, *rest):
    if round_down:
        wd_ref, a_ref, wd16_ref = rest
        wd16_ref[...] = wd_ref[...].astype(BF16)
    else:
        a_ref, = rest
    wg = wg_ref[...].astype(BF16)
    wu = wu_ref[...].astype(BF16)
    sub = min(tm, TILES.row_sub_up)
    for r in range(tm // sub):
        rows = slice(r * sub, (r + 1) * sub)
        h = h_ref[rows, :]
        a_ref[rows, :] = (_silu(_dot(h, wg)) * _dot(h, wu)).astype(BF16)


def _ffn_down_kernel(tm, round_next, a_ref, wd_ref, x_ref, gt_ref, g_ref, *rest):
    if round_next:
        win_ref, wout_ref, o_ref, win16_ref, wout16_ref = rest
        win16_ref[...] = win_ref[...].astype(BF16)
        wout16_ref[...] = wout_ref[...].astype(BF16)
    else:
        o_ref, = rest
    gain = gt_ref[...] * g_ref[...]
    sub = min(tm, TILES.row_sub)
    for r in range(tm // sub):
        rows = slice(r * sub, (r + 1) * sub)
        o_ref[rows, :] = x_ref[rows, :] + _rms(_dot(a_ref[rows, :], wd_ref[...])) * gain


def _ffn(h2, w_gate, w_up, w_down, layer, x, mod, row_of_block, g_post, tm_up, tm_down, w_down16=None,
         next_proj=None, tf=TILES.ffn_cols):
    m = x.shape[0]
    hidden = w_gate.shape[-1]
    nf = hidden // tf
    round_down = w_down16 is None
    in_specs = [
        pl.BlockSpec((tm_up, D_MODEL), lambda i, f: (i, 0)),
        pl.BlockSpec((None, D_MODEL, tf), lambda i, f: (layer, 0, f)),
        pl.BlockSpec((None, D_MODEL, tf), lambda i, f: (layer, 0, f)),
    ]
    out_shape = [jax.ShapeDtypeStruct((m, hidden), BF16)]
    out_specs = [pl.BlockSpec((tm_up, tf), lambda i, f: (i, f))]
    args = [h2, w_gate, w_up]
    if round_down:
        slab = hidden // ((m // tm_up) * nf)
        assert slab % BF16_SUBLANES == 0 and slab * (m // tm_up) * nf == hidden
        in_specs.append(pl.BlockSpec((None, slab, D_MODEL), lambda i, f: (layer, i * nf + f, 0)))
        out_shape.append(jax.ShapeDtypeStruct((hidden, D_MODEL), BF16))
        out_specs.append(pl.BlockSpec((slab, D_MODEL), lambda i, f: (i * nf + f, 0)))
        args.append(w_down)
    outs = pl.pallas_call(
        functools.partial(_ffn_up_kernel, tm_up, round_down),
        out_shape=out_shape,
        grid=(m // tm_up, nf),
        in_specs=in_specs,
        out_specs=out_specs,
        compiler_params=_params(("parallel", "arbitrary")),
        name="ffn_up",
    )(*args)
    act = outs[0]
    if round_down:
        w_down16 = outs[1]
    in_specs = [
        pl.BlockSpec((tm_down, hidden), lambda i: (i, 0)),
        _resident((hidden, D_MODEL), lambda i: (0, 0)),
        pl.BlockSpec((tm_down, D_MODEL), lambda i: (i, 0)),
        _mod_spec(5, row_of_block),
        _vec_spec(D_MODEL),
    ]
    out_shape = [jax.ShapeDtypeStruct((m, D_MODEL), F32)]
    out_specs = [pl.BlockSpec((tm_down, D_MODEL), lambda i: (i, 0))]
    args = [act, w_down16, x, mod, g_post]
    if next_proj is not None:
        slab = D_MODEL // (m // tm_down)
        assert slab % BF16_SUBLANES == 0 and slab * (m // tm_down) == D_MODEL
        for w in next_proj:
            cols = w.shape[-1]
            in_specs.append(pl.BlockSpec((None, slab, cols), lambda i: (layer + 1, i, 0)))
            out_shape.append(jax.ShapeDtypeStruct((D_MODEL, cols), BF16))
            out_specs.append(pl.BlockSpec((slab, cols), lambda i: (i, 0)))
            args.append(w)
    outs = pl.pallas_call(
        functools.partial(_ffn_down_kernel, tm_down, next_proj is not None),
        out_shape=out_shape,
        grid=(m // tm_down,),
        in_specs=in_specs,
        out_specs=out_specs,
        compiler_params=_params(("parallel",)),
        name="ffn_down",
    )(*args)
    return outs[0], w_down16, tuple(outs[1:])


def _rope_tables(seq):
    rows = seq // GRID_W
    inv_freq = jnp.power(ROPE_BASE, -jnp.arange(ROPE_PAIRS, dtype=F32) / ROPE_PAIRS)
    ang_row = jnp.arange(rows, dtype=jnp.int32).astype(F32)[:, None] * inv_freq
    ang_col = jnp.arange(GRID_W, dtype=jnp.int32).astype(F32)[:, None] * inv_freq
    by_row = lambda t: jnp.repeat(t, GRID_W, axis=0)
    by_col = lambda t: jnp.tile(t, (rows, 1))
    cos_r, sin_r = by_row(jnp.cos(ang_row)), by_row(jnp.sin(ang_row))
    cos_c, sin_c = by_col(jnp.cos(ang_col)), by_col(jnp.sin(ang_col))
    cos_t = jnp.concatenate([cos_r, cos_r, cos_c, cos_c], axis=-1)
    sin_t = jnp.concatenate([-sin_r, sin_r, -sin_c, sin_c], axis=-1)
    return cos_t, sin_t


def kernel(x, c, ctx, c_ctx, w_ada, b_ada, g_mix_pre, g_mix_post, g_ffn_pre, g_ffn_post, w_in, q_norm, k_norm,
           sink, conv_w, w_out, w_gate, w_up, w_down):
    batch, seq, d = x.shape
    ctx_len = ctx.shape[1]
    depth = w_ada.shape[0]
    assert d == D_MODEL and batch < MOD_ROWS and ctx_len % Q_BLOCK == 0
    assert seq % TILES.row_in == 0 and (batch * seq) % TILES.row_up == 0 and seq % GRID_W == 0

    tm, tm_in, tm_up = TILES.row, TILES.row_in, TILES.row_up
    tm_ctx = ctx_len
    tm_ctx_ffn = min(tm, batch * ctx_len)
    lat_row = lambda i: i // (seq // tm)
    lat_row_in = lambda i: i // (seq // tm_in)
    ctx_row = lambda i: batch

    cond = jnp.zeros((MOD_ROWS, d), F32).at[:batch].set(c).at[batch].set(c_ctx)
    as_table = lambda t: t.reshape(MOD_ROWS * N_MOD, 1, d)
    mod = as_table(_ada(cond, w_ada, b_ada, layers=1)[0])
    tables = _rope_tables(seq)
    w_in16, w_out16 = w_in[0].astype(BF16), None

    xl = x.reshape(batch * seq, d)
    xc = ctx.reshape(batch * ctx_len, d)
    vec = lambda a, l: a[l].reshape(1, -1)
    for l in range(depth):
        g_pre, g_post = vec(g_mix_pre, l), vec(g_mix_post, l)
        g_ffn, g_ffn_out = vec(g_ffn_pre, l), vec(g_ffn_post, l)
        qn, kn = vec(q_norm, l), vec(k_norm, l)
        update_ctx = l < depth - 1

        ctx_proj = _inproj(xc, mod, ctx_row, g_pre, w_in16, qn, kn, None, tm_ctx, ctx_len, kv_only=not update_ctx)
        kv_c = ctx_proj[0]
        lat_proj = _inproj(xl, mod, lat_row_in, g_pre, w_in16, qn, kn, tables, tm_in, seq,
                           round_out=(w_out, l) if w_out16 is None else None)
        kv, qa, qb, z, gp = lat_proj[:5]
        if w_out16 is None:
            w_out16 = lat_proj[5]
        oa, mod_next = _attn_a(qa, kv, kv_c, batch, seq, ctx_len,
                               ada_next=(cond, w_ada, b_ada, l + 1) if l + 1 < depth else None)
        ob = _attn_b(qb, kv, kv_c, sink[l], batch, seq, ctx_len)
        xl, h2 = _outproj(oa, ob, z, gp, conv_w[l], w_out16, xl, mod, lat_row, g_post, g_ffn, tm, seq)
        xl, w_down16, next16 = _ffn(h2, w_gate, w_up, w_down, l, xl, mod, lat_row, g_ffn_out, tm_up, tm,
                                    next_proj=(w_in, w_out) if l + 1 < depth else None)
        if update_ctx:
            _, qa_c, qb_c, z_c, gp_c = ctx_proj
            oa_c, ob_c = _attn_ctx(qa_c, qb_c, kv_c, sink[l], batch, ctx_len)
            xc, h2_c = _outproj(oa_c, ob_c, z_c, gp_c, conv_w[l], w_out16, xc, mod, ctx_row, g_post, g_ffn,
                                tm_ctx, ctx_len)
            xc, _, _ = _ffn(h2_c, w_gate, w_up, w_down, l, xc, mod, ctx_row, g_ffn_out,
                            min(tm_up, batch * ctx_len), tm_ctx_ffn, w_down16)
        if l + 1 < depth:
            w_in16, w_out16 = next16
            mod = as_table(mod_next)
    return xl.reshape(batch, seq, d)
```

```python
import functools
from typing import NamedTuple

import jax
import jax.numpy as jnp
from jax import lax
from jax.experimental import pallas as pl
from jax.experimental.pallas import tpu as pltpu

F32 = jnp.float32
BF16 = jnp.bfloat16

D_MODEL = 2048
HEAD_DIM = 128
GRID_W = 64
A_Q_HEADS = 6
A_KV_HEADS = 2
B_Q_HEADS = 6
B_KV_HEADS = 2
GROUP = A_Q_HEADS // A_KV_HEADS
Q_COLS = A_Q_HEADS * HEAD_DIM
CONV_CH = D_MODEL - 2 * Q_COLS
WINDOW = 128
Q_BLOCK = 128
ROPE_BASE = 10000.0
ROPE_PAIRS = HEAD_DIM // 4
NORM_EPS = 1e-6
NEG_INF = -1e30
ATTN_SCALE = HEAD_DIM ** -0.5
LOG2_E = 1.4426950408889634
SCORE_SCALE = ATTN_SCALE * LOG2_E
N_MOD = 6
MOD_ROWS = 8

KV_COLS = 4 * A_KV_HEADS * HEAD_DIM
OFF_AQ = KV_COLS
OFF_BQ = OFF_AQ + Q_COLS
OFF_CONV = OFF_BQ + Q_COLS
IN_COLS = OFF_CONV + 3 * CONV_CH

V7X_VMEM_BYTES = 64 * 1024 * 1024
VMEM_LIMIT = V7X_VMEM_BYTES - 8 * 1024 * 1024
BF16_SUBLANES = 16


class _Tiles(NamedTuple):
    row: int = 512
    row_in: int = 1024
    row_up: int = 2048
    row_sub: int = 256
    row_sub_up: int = 128
    ffn_cols: int = 512
    ada_cols: int = 1024
    attn_q: int = 256
    attn_k: int = 512
    attn_a_blocks: int = 4
    attn_b_blocks: int = 4


TILES = _Tiles()


def _rms(t):
    return t * lax.rsqrt(jnp.mean(t * t, axis=-1, keepdims=True) + NORM_EPS)


def _silu(t):
    return t * (1.0 / (1.0 + jnp.exp(-t)))


def _dot(a, b):
    return jnp.dot(a, b, preferred_element_type=F32)


def _dot_nt(a, b):
    return lax.dot_general(a, b, (((1,), (1,)), ((), ())), preferred_element_type=F32)


def _params(semantics):
    return pltpu.CompilerParams(dimension_semantics=semantics, vmem_limit_bytes=VMEM_LIMIT)


def _resident(shape, index_map):
    return pl.BlockSpec(shape, index_map, pipeline_mode=pl.Buffered(1))


def _ada_slab(s_ref, w_ref, b_ref):
    return _dot(_silu(s_ref[...]).astype(BF16), w_ref[...].astype(BF16)) + b_ref[...]


def _ada_kernel(s_ref, w_ref, b_ref, o_ref):
    o_ref[0] = _ada_slab(s_ref, w_ref.at[0], b_ref.at[0])


def _ada(cond, w_ada, b_ada, layers, tn=TILES.ada_cols):
    _, d, n = w_ada.shape
    return pl.pallas_call(
        _ada_kernel,
        out_shape=jax.ShapeDtypeStruct((layers, MOD_ROWS, n), F32),
        grid=(layers, n // tn),
        in_specs=[
            pl.BlockSpec((MOD_ROWS, d), lambda l, j: (0, 0)),
            pl.BlockSpec((1, d, tn), lambda l, j: (l, 0, j)),
            pl.BlockSpec((1, 1, tn), lambda l, j: (l, 0, j)),
        ],
        out_specs=pl.BlockSpec((1, MOD_ROWS, tn), lambda l, j: (l, 0, j)),
        compiler_params=_params(("parallel", "parallel")),
        name="ada",
    )(cond, w_ada, b_ada.reshape(-1, 1, n))


def _mod_spec(chunk, row_of_block):
    return pl.BlockSpec((None, 1, D_MODEL), lambda i: (row_of_block(i) * N_MOD + chunk, 0, 0))


def _vec_spec(n):
    return pl.BlockSpec((1, n), lambda i: (0, 0))


def _inproj_kernel(rope, kv_only, tm, x_ref, g_ref, sc_ref, sh_ref, w_ref, qn_ref, kn_ref, *rest):
    if rope:
        cos_ref, sin_ref = rest[:2]
        rest = rest[2:]
    if kv_only:
        kv_ref, = rest
    else:
        kv_ref, qa_ref, qb_ref, z_ref, gp_ref = rest
    sub = min(tm, TILES.row_sub)
    pre_gain = g_ref[...] * (1.0 + sc_ref[...])
    shift = sh_ref[...]
    k_gain = kn_ref[...] * SCORE_SCALE
    q_gain = qn_ref[...]
    lane = lax.broadcasted_iota(jnp.int32, (sub, HEAD_DIM), 1)
    first_half = (lane & ROPE_PAIRS) == 0

    for r in range(tm // sub):
        rows = slice(r * sub, (r + 1) * sub)
        if rope:
            cos = cos_ref[rows, :]
            sin = sin_ref[rows, :]

            def rot(t):
                partner = jnp.where(first_half, pltpu.roll(t, HEAD_DIM - ROPE_PAIRS, 1),
                                    pltpu.roll(t, ROPE_PAIRS, 1))
                return t * cos + partner * sin
        else:
            def rot(t):
                return t

        hb = (_rms(x_ref[rows, :]) * pre_gain + shift).astype(BF16)

        def proj(c0, n):
            return _dot(hb, w_ref[:, c0:c0 + n])

        def head(t, k):
            return t[:, k * HEAD_DIM:(k + 1) * HEAD_DIM]

        def put(ref, k, t):
            ref[rows, k * HEAD_DIM:(k + 1) * HEAD_DIM] = t.astype(BF16)

        pkv = proj(0, KV_COLS)
        for k in range(A_KV_HEADS):
            put(kv_ref, k, rot(_rms(head(pkv, k)) * k_gain))
            put(kv_ref, A_KV_HEADS + k, head(pkv, A_KV_HEADS + k))
            put(kv_ref, 2 * A_KV_HEADS + k, rot(head(pkv, 2 * A_KV_HEADS + k)) * SCORE_SCALE)
            put(kv_ref, 3 * A_KV_HEADS + k, head(pkv, 3 * A_KV_HEADS + k))
        if kv_only:
            continue

        pq = proj(OFF_AQ, Q_COLS)
        for k in range(A_Q_HEADS):
            put(qa_ref, k, rot(_rms(head(pq, k)) * q_gain))
        pq = proj(OFF_BQ, Q_COLS)
        for k in range(B_Q_HEADS):
            put(qb_ref, k, rot(head(pq, k)))

        z_ref[rows, :] = (proj(OFF_CONV, CONV_CH) * proj(OFF_CONV + 2 * CONV_CH, CONV_CH)).astype(BF16)
        gp_ref[rows, :] = proj(OFF_CONV + CONV_CH, CONV_CH).astype(BF16)


def _inproj(x, mod, row_of_block, g_pre, w_in, q_norm, k_norm, rope_tables, tm, seq_len, kv_only=False):
    m = x.shape[0]
    rope = rope_tables is not None
    cols = KV_COLS if kv_only else IN_COLS
    in_specs = [
        pl.BlockSpec((tm, D_MODEL), lambda i: (i, 0)),
        _vec_spec(D_MODEL),
        _mod_spec(1, row_of_block),
        _mod_spec(0, row_of_block),
        _resident((D_MODEL, cols), lambda i: (0, 0)),
        _vec_spec(HEAD_DIM),
        _vec_spec(HEAD_DIM),
    ]
    args = [x, g_pre, mod, mod, w_in, q_norm, k_norm]
    if rope:
        blocks_per_seq = seq_len // tm
        in_specs += [pl.BlockSpec((tm, HEAD_DIM), lambda i: (i % blocks_per_seq, 0))] * 2
        args += list(rope_tables)
    widths = (KV_COLS,) if kv_only else (KV_COLS, Q_COLS, Q_COLS, CONV_CH, CONV_CH)
    return pl.pallas_call(
        functools.partial(_inproj_kernel, rope, kv_only, tm),
        out_shape=[jax.ShapeDtypeStruct((m, w), BF16) for w in widths],
        grid=(m // tm,),
        in_specs=in_specs,
        out_specs=[pl.BlockSpec((tm, w), lambda i: (i, 0)) for w in widths],
        compiler_params=_params(("parallel",)),
        name=("inproj_rope" if rope else "inproj") + ("_kv" if kv_only else ""),
    )(*args)


def _stack_heads(q_ref, kvh, rows=slice(None)):
    base = kvh * GROUP * HEAD_DIM
    return jnp.concatenate(
        [q_ref[rows, base + g * HEAD_DIM: base + (g + 1) * HEAD_DIM] for g in range(GROUP)], axis=0)


def _unstack_heads(o_ref, kvh, o, tq, rows=slice(None)):
    base = kvh * GROUP * HEAD_DIM
    for g in range(GROUP):
        o_ref[rows, base + g * HEAD_DIM: base + (g + 1) * HEAD_DIM] = o[g * tq:(g + 1) * tq].astype(BF16)


def _sink_column(sink_ref, kvh, tq):
    grp = lax.broadcasted_iota(jnp.int32, (GROUP * tq, 1), 0) // tq
    col = jnp.full((GROUP * tq, 1), sink_ref[kvh * GROUP], F32)
    for g in range(1, GROUP):
        col = jnp.where(grp == g, sink_ref[kvh * GROUP + g], col)
    return col * LOG2_E


def _softmax_pv(scores, values, sink_col=None):
    tiles = [s[:, t * HEAD_DIM:(t + 1) * HEAD_DIM] for s in scores for t in range(s.shape[1] // HEAD_DIM)]
    m = functools.reduce(jnp.maximum, tiles).max(axis=-1, keepdims=True)
    if sink_col is not None:
        m = jnp.maximum(m, sink_col)
    acc = None
    for s, v in zip(scores, values):
        v1 = jnp.concatenate([v, jnp.ones_like(v)], axis=1)
        a = _dot(jnp.exp2(s - m).astype(BF16), v1)
        acc = a if acc is None else acc + a
    den = acc[:, HEAD_DIM:]
    if sink_col is not None:
        den = den + jnp.exp2(sink_col - m)
    return acc[:, :HEAD_DIM] / den


def _values_with_ones(v):
    return jnp.concatenate([v.T, jnp.ones((BF16_SUBLANES, v.shape[0]), BF16)], axis=0)


def _attn_a_kernel(tq, tk, nblk, ada_next, q_ref, kc_ref, vc_ref, k_ref, v_ref, *rest):
    if ada_next:
        cond_ref, wa_ref, ba_ref, o_ref, mod_ref, acc_scr = rest
        mod_ref[...] = _ada_slab(cond_ref, wa_ref, ba_ref)
    else:
        o_ref, acc_scr = rest
    nc = k_ref.shape[0] // tk

    def scores(stage):
        j, c = stage
        k, v = (kc_ref[...], vc_ref[...]) if c < 0 else (k_ref[c * tk:(c + 1) * tk, :], v_ref[c * tk:(c + 1) * tk, :])
        return _dot_nt(k, qs[j]), v

    stages = [(j, c) for j in range(nblk) for c in range(-1, nc)]
    qs = [_stack_heads(q_ref, 0, slice(j * tq, (j + 1) * tq)) for j in range(nblk)]
    m = None
    s, v = scores(stages[0])
    for n, (j, c) in enumerate(stages):
        if n + 1 < len(stages):
            s_next, v_next = scores(stages[n + 1])
        acc = acc_scr.at[j]
        m_new = s.max(axis=0, keepdims=True)
        if c >= 0:
            m_new = jnp.maximum(m, m_new)
        pv = _dot(_values_with_ones(v), jnp.exp2(s - m_new).astype(BF16))
        if c < 0:
            acc[...] = pv
        else:
            acc[...] = jnp.exp2(m - m_new) * acc[...] + pv
        m = m_new
        if c == nc - 1:
            a = acc[...]
            o = a[:HEAD_DIM] / a[HEAD_DIM:HEAD_DIM + 1]
            for g in range(GROUP):
                o_ref[j * tq:(j + 1) * tq, g * HEAD_DIM:(g + 1) * HEAD_DIM] = (
                    o[:, g * tq:(g + 1) * tq].T.astype(BF16))
        if n + 1 < len(stages):
            s, v = s_next, v_next


def _attn_a(qa, kv, kv_ctx, batch, seq, ctx_len, ada_next=None, tq=TILES.attn_q, tk=TILES.attn_k,
            nblk=TILES.attn_a_blocks):
    nq = seq // (tq * nblk)
    gw = GROUP * HEAD_DIM
    in_specs = [
        pl.BlockSpec((tq * nblk, gw), lambda b, h, i: (b * nq + i, h)),
        pl.BlockSpec((ctx_len, HEAD_DIM), lambda b, h, i: (b, h)),
        pl.BlockSpec((ctx_len, HEAD_DIM), lambda b, h, i: (b, A_KV_HEADS + h)),
        pl.BlockSpec((seq, HEAD_DIM), lambda b, h, i: (b, h)),
        pl.BlockSpec((seq, HEAD_DIM), lambda b, h, i: (b, A_KV_HEADS + h)),
    ]
    out_shape = [jax.ShapeDtypeStruct(qa.shape, BF16)]
    out_specs = [pl.BlockSpec((tq * nblk, gw), lambda b, h, i: (b * nq + i, h))]
    args = [qa, kv_ctx, kv_ctx, kv, kv]
    if ada_next is not None:
        cond, w_ada, b_ada, layer = ada_next
        n = w_ada.shape[-1]
        nsteps = batch * A_KV_HEADS * nq
        slab = n // nsteps
        assert slab % HEAD_DIM == 0 and slab * nsteps == n
        step = lambda b, h, i: (b * A_KV_HEADS + h) * nq + i
        in_specs += [
            pl.BlockSpec((MOD_ROWS, D_MODEL), lambda b, h, i: (0, 0)),
            pl.BlockSpec((None, D_MODEL, slab), lambda b, h, i: (layer, 0, step(b, h, i))),
            pl.BlockSpec((None, 1, slab), lambda b, h, i: (layer, 0, step(b, h, i))),
        ]
        out_shape.append(jax.ShapeDtypeStruct((MOD_ROWS, n), F32))
        out_specs.append(pl.BlockSpec((MOD_ROWS, slab), lambda b, h, i: (0, step(b, h, i))))
        args += [cond, w_ada, b_ada.reshape(-1, 1, n)]
    outs = pl.pallas_call(
        functools.partial(_attn_a_kernel, tq, tk, nblk, ada_next is not None),
        out_shape=out_shape,
        grid=(batch, A_KV_HEADS, nq),
        in_specs=in_specs,
        out_specs=out_specs,
        scratch_shapes=[pltpu.VMEM((nblk, HEAD_DIM + BF16_SUBLANES, GROUP * tq), F32)],
        compiler_params=_params(("parallel", "parallel", "arbitrary")),
        name="attn_a",
    )(*args)
    return outs[0], (outs[1] if ada_next is not None else None)


def _attn_b_kernel(seq, nblk, sink_ref, q_ref, kvc_ref, kv_ref, bias_ref, o_ref):
    span = Q_BLOCK + 2 * WINDOW
    last = seq // Q_BLOCK - 1
    work = []
    for j in range(nblk):
        blk = pl.program_id(1) * nblk + j
        qrows = slice(j * Q_BLOCK, (j + 1) * Q_BLOCK)
        start = pl.multiple_of(jnp.clip(blk * Q_BLOCK - WINDOW, 0, seq - span), Q_BLOCK)
        bias = bias_ref[jnp.where(blk == 0, 0, jnp.where(blk == last, 2, 1))]
        for kvh in range(B_KV_HEADS):
            kc = slice(kvh * HEAD_DIM, (kvh + 1) * HEAD_DIM)
            q = _stack_heads(q_ref, kvh, qrows)
            s_win = _dot_nt(q, kv_ref[pl.ds(start, span), kc]) + bias
            s_ctx = _dot_nt(q, kvc_ref[:, kc])
            work.append((qrows, kvh, start, s_win, s_ctx))
    for qrows, kvh, start, s_win, s_ctx in work:
        vc = slice((B_KV_HEADS + kvh) * HEAD_DIM, (B_KV_HEADS + kvh + 1) * HEAD_DIM)
        o = _softmax_pv([s_win, s_ctx], [kv_ref[pl.ds(start, span), vc], kvc_ref[:, vc]],
                        _sink_column(sink_ref, kvh, Q_BLOCK))
        _unstack_heads(o_ref, kvh, o, Q_BLOCK, qrows)


def _band_bias():
    span = Q_BLOCK + 2 * WINDOW
    r = jnp.arange(GROUP * Q_BLOCK, dtype=jnp.int32)[:, None] % Q_BLOCK
    c = jnp.arange(span, dtype=jnp.int32)[None, :]
    off = jnp.arange(3, dtype=jnp.int32)[:, None, None] * WINDOW
    return jnp.where(jnp.abs(off + r - c) <= WINDOW, 0.0, NEG_INF).astype(F32)


def _attn_b(qb, kv, kv_ctx, sink, batch, seq, ctx_len, nblk=TILES.attn_b_blocks):
    nq = seq // (Q_BLOCK * nblk)
    bw = 2 * B_KV_HEADS * HEAD_DIM
    span = Q_BLOCK + 2 * WINDOW
    return pl.pallas_call(
        functools.partial(_attn_b_kernel, seq, nblk),
        out_shape=jax.ShapeDtypeStruct(qb.shape, BF16),
        grid_spec=pltpu.PrefetchScalarGridSpec(
            num_scalar_prefetch=1,
            grid=(batch, nq),
            in_specs=[
                pl.BlockSpec((Q_BLOCK * nblk, Q_COLS), lambda b, i, s: (b * nq + i, 0)),
                pl.BlockSpec((ctx_len, bw), lambda b, i, s: (b, 1)),
                pl.BlockSpec((seq, bw), lambda b, i, s: (b, 1)),
                pl.BlockSpec((3, GROUP * Q_BLOCK, span), lambda b, i, s: (0, 0, 0)),
            ],
            out_specs=pl.BlockSpec((Q_BLOCK * nblk, Q_COLS), lambda b, i, s: (b * nq + i, 0)),
        ),
        compiler_params=_params(("parallel", "arbitrary")),
        name="attn_b",
    )(sink, qb, kv_ctx, kv, _band_bias())


def _attn_ctx_kernel(ctx_len, sink_ref, qa_ref, qb_ref, kv_ref, oa_ref, ob_ref):
    for kvh in range(A_KV_HEADS):
        k = kv_ref[:, kvh * HEAD_DIM:(kvh + 1) * HEAD_DIM]
        v = kv_ref[:, (A_KV_HEADS + kvh) * HEAD_DIM:(A_KV_HEADS + kvh + 1) * HEAD_DIM]
        o = _softmax_pv([_dot_nt(_stack_heads(qa_ref, kvh), k)], [v])
        _unstack_heads(oa_ref, kvh, o, ctx_len)
    off = 2 * A_KV_HEADS
    for kvh in range(B_KV_HEADS):
        k = kv_ref[:, (off + kvh) * HEAD_DIM:(off + kvh + 1) * HEAD_DIM]
        v = kv_ref[:, (off + B_KV_HEADS + kvh) * HEAD_DIM:(off + B_KV_HEADS + kvh + 1) * HEAD_DIM]
        o = _softmax_pv([_dot_nt(_stack_heads(qb_ref, kvh), k)], [v], _sink_column(sink_ref, kvh, ctx_len))
        _unstack_heads(ob_ref, kvh, o, ctx_len)


def _attn_ctx(qa, qb, kv_ctx, sink, batch, ctx_len):
    q_spec = pl.BlockSpec((ctx_len, Q_COLS), lambda b, s: (b, 0))
    return pl.pallas_call(
        functools.partial(_attn_ctx_kernel, ctx_len),
        out_shape=[jax.ShapeDtypeStruct(qa.shape, BF16), jax.ShapeDtypeStruct(qb.shape, BF16)],
        grid_spec=pltpu.PrefetchScalarGridSpec(
            num_scalar_prefetch=1,
            grid=(batch,),
            in_specs=[q_spec, q_spec, pl.BlockSpec((ctx_len, KV_COLS), lambda b, s: (b, 0))],
            out_specs=[q_spec, q_spec],
        ),
        compiler_params=_params(("parallel",)),
        name="attn_ctx",
    )(sink, qa, qb, kv_ctx)


def _outproj_kernel(tm, blocks_per_seq, oa_ref, ob_ref, z_ref, zprev_ref, znext_ref, gp_ref, cw_ref, w_ref,
                    x_ref, gt_ref, gpost_ref, gffn_ref, sc_ref, sh_ref, xo_ref, h2_ref, oc_scr):
    i = pl.program_id(0)
    pos = i % blocks_per_seq
    z = z_ref[...].astype(F32)
    prev_row = jnp.where(pos == 0, 0.0, zprev_ref[BF16_SUBLANES - 1:BF16_SUBLANES, :].astype(F32))
    next_row = jnp.where(pos == blocks_per_seq - 1, 0.0, znext_ref[0:1, :].astype(F32))
    row = lax.broadcasted_iota(jnp.int32, z.shape, 0)
    z_prev = jnp.where(row == 0, prev_row, pltpu.roll(z, 1, 0))
    z_next = jnp.where(row == tm - 1, next_row, pltpu.roll(z, tm - 1, 0))
    cw = cw_ref[...]
    oc = gp_ref[...].astype(F32) * (z_prev * cw[0:1] + z * cw[1:2] + z_next * cw[2:3])
    oc_scr[...] = oc.astype(BF16)

    post_gain = gt_ref[...] * gpost_ref[...]
    ffn_gain = gffn_ref[...] * (1.0 + sc_ref[...])
    shift = sh_ref[...]
    sub = min(tm, TILES.row_sub)
    for r in range(tm // sub):
        rows = slice(r * sub, (r + 1) * sub)
        y = _dot(oa_ref[rows, :], w_ref[0:Q_COLS, :])
        y += _dot(ob_ref[rows, :], w_ref[Q_COLS:2 * Q_COLS, :])
        y += _dot(oc_scr[rows, :], w_ref[2 * Q_COLS:, :])
        x = x_ref[rows, :] + _rms(y) * post_gain
        xo_ref[rows, :] = x
        h2_ref[rows, :] = (_rms(x) * ffn_gain + shift).astype(BF16)


def _outproj(oa, ob, z, gp, conv_w, w_out, x, mod, row_of_block, g_post, g_ffn, tm, seq_len):
    m = x.shape[0]
    blocks_per_seq = seq_len // tm
    halo = tm // BF16_SUBLANES
    last_halo = m // BF16_SUBLANES - 1
    row_spec = lambda w: pl.BlockSpec((tm, w), lambda i: (i, 0))
    return pl.pallas_call(
        functools.partial(_outproj_kernel, tm, blocks_per_seq),
        out_shape=[jax.ShapeDtypeStruct((m, D_MODEL), F32), jax.ShapeDtypeStruct((m, D_MODEL), BF16)],
        grid=(m // tm,),
        in_specs=[
            row_spec(Q_COLS),
            row_spec(Q_COLS),
            row_spec(CONV_CH),
            pl.BlockSpec((BF16_SUBLANES, CONV_CH), lambda i: (jnp.maximum(i * halo - 1, 0), 0)),
            pl.BlockSpec((BF16_SUBLANES, CONV_CH), lambda i: (jnp.minimum((i + 1) * halo, last_halo), 0)),
            row_spec(CONV_CH),
            pl.BlockSpec(conv_w.shape, lambda i: (0, 0)),
            _resident((D_MODEL, D_MODEL), lambda i: (0, 0)),
            row_spec(D_MODEL),
            _mod_spec(2, row_of_block),
            _vec_spec(D_MODEL),
            _vec_spec(D_MODEL),
            _mod_spec(4, row_of_block),
            _mod_spec(3, row_of_block),
        ],
        out_specs=[row_spec(D_MODEL), row_spec(D_MODEL)],
        scratch_shapes=[pltpu.VMEM((tm, CONV_CH), BF16)],
        compiler_params=_params(("parallel",)),
        name="outproj",
    )(oa, ob, z, z, z, gp, conv_w, w_out, x, mod, g_post, g_ffn, mod, mod)


def _ffn_up_kernel(tm, round_down, h_ref, wg_ref, wu_ref, *rest):
    if round_down:
        wd_ref, a_ref, wd16_ref = rest
        wd16_ref[...] = wd_ref[...].astype(BF16)
    else:
        a_ref, = rest
    wg = wg_ref[...].astype(BF16)
    wu = wu_ref[...].astype(BF16)
    sub = min(tm, TILES.row_sub_up)
    for r in range(tm // sub):
        rows = slice(r * sub, (r + 1) * sub)
        h = h_ref[rows, :]
        a_ref[rows, :] = (_silu(_dot(h, wg)) * _dot(h, wu)).astype(BF16)


def _ffn_down_kernel(tm, round_next, a_ref, wd_ref, x_ref, gt_ref, g_ref, *rest):
    if round_next:
        win_ref, wout_ref, o_ref, win16_ref, wout16_ref = rest
        win16_ref[...] = win_ref[...].astype(BF16)
        wout16_ref[...] = wout_ref[...].astype(BF16)
    else:
        o_ref, = rest
    gain = gt_ref[...] * g_ref[...]
    sub = min(tm, TILES.row_sub)
    for r in range(tm // sub):
        rows = slice(r * sub, (r + 1) * sub)
        o_ref[rows, :] = x_ref[rows, :] + _rms(_dot(a_ref[rows, :], wd_ref[...])) * gain


def _ffn(h2, w_gate, w_up, w_down, layer, x, mod, row_of_block, g_post, tm_up, tm_down, w_down16=None,
         next_proj=None, tf=TILES.ffn_cols):
    m = x.shape[0]
    hidden = w_gate.shape[-1]
    nf = hidden // tf
    round_down = w_down16 is None
    in_specs = [
        pl.BlockSpec((tm_up, D_MODEL), lambda i, f: (i, 0)),
        pl.BlockSpec((None, D_MODEL, tf), lambda i, f: (layer, 0, f)),
        pl.BlockSpec((None, D_MODEL, tf), lambda i, f: (layer, 0, f)),
    ]
    out_shape = [jax.ShapeDtypeStruct((m, hidden), BF16)]
    out_specs = [pl.BlockSpec((tm_up, tf), lambda i, f: (i, f))]
    args = [h2, w_gate, w_up]
    if round_down:
        slab = hidden // ((m // tm_up) * nf)
        assert slab % BF16_SUBLANES == 0 and slab * (m // tm_up) * nf == hidden
        in_specs.append(pl.BlockSpec((None, slab, D_MODEL), lambda i, f: (layer, i * nf + f, 0)))
        out_shape.append(jax.ShapeDtypeStruct((hidden, D_MODEL), BF16))
        out_specs.append(pl.BlockSpec((slab, D_MODEL), lambda i, f: (i * nf + f, 0)))
        args.append(w_down)
    outs = pl.pallas_call(
        functools.partial(_ffn_up_kernel, tm_up, round_down),
        out_shape=out_shape,
        grid=(m // tm_up, nf),
        in_specs=in_specs,
        out_specs=out_specs,
        compiler_params=_params(("parallel", "arbitrary")),
        name="ffn_up",
    )(*args)
    act = outs[0]
    if round_down:
        w_down16 = outs[1]
    in_specs = [
        pl.BlockSpec((tm_down, hidden), lambda i: (i, 0)),
        _resident((hidden, D_MODEL), lambda i: (0, 0)),
        pl.BlockSpec((tm_down, D_MODEL), lambda i: (i, 0)),
        _mod_spec(5, row_of_block),
        _vec_spec(D_MODEL),
    ]
    out_shape = [jax.ShapeDtypeStruct((m, D_MODEL), F32)]
    out_specs = [pl.BlockSpec((tm_down, D_MODEL), lambda i: (i, 0))]
    args = [act, w_down16, x, mod, g_post]
    if next_proj is not None:
        slab = D_MODEL // (m // tm_down)
        assert slab % BF16_SUBLANES == 0 and slab * (m // tm_down) == D_MODEL
        for w in next_proj:
            cols = w.shape[-1]
            in_specs.append(pl.BlockSpec((None, slab, cols), lambda i: (layer + 1, i, 0)))
            out_shape.append(jax.ShapeDtypeStruct((D_MODEL, cols), BF16))
            out_specs.append(pl.BlockSpec((slab, cols), lambda i: (i, 0)))
            args.append(w)
    outs = pl.pallas_call(
        functools.partial(_ffn_down_kernel, tm_down, next_proj is not None),
        out_shape=out_shape,
        grid=(m // tm_down,),
        in_specs=in_specs,
        out_specs=out_specs,
        compiler_params=_params(("parallel",)),
        name="ffn_down",
    )(*args)
    return outs[0], w_down16, tuple(outs[1:])


def _rope_tables(seq):
    rows = seq // GRID_W
    inv_freq = jnp.power(ROPE_BASE, -jnp.arange(ROPE_PAIRS, dtype=F32) / ROPE_PAIRS)
    ang_row = jnp.arange(rows, dtype=jnp.int32).astype(F32)[:, None] * inv_freq
    ang_col = jnp.arange(GRID_W, dtype=jnp.int32).astype(F32)[:, None] * inv_freq
    by_row = lambda t: jnp.repeat(t, GRID_W, axis=0)
    by_col = lambda t: jnp.tile(t, (rows, 1))
    cos_r, sin_r = by_row(jnp.cos(ang_row)), by_row(jnp.sin(ang_row))
    cos_c, sin_c = by_col(jnp.cos(ang_col)), by_col(jnp.sin(ang_col))
    cos_t = jnp.concatenate([cos_r, cos_r, cos_c, cos_c], axis=-1)
    sin_t = jnp.concatenate([-sin_r, sin_r, -sin_c, sin_c], axis=-1)
    return cos_t, sin_t


def kernel(x, c, ctx, c_ctx, w_ada, b_ada, g_mix_pre, g_mix_post, g_ffn_pre, g_ffn_post, w_in, q_norm, k_norm,
           sink, conv_w, w_out, w_gate, w_up, w_down):
    batch, seq, d = x.shape
    ctx_len = ctx.shape[1]
    depth = w_ada.shape[0]
    assert d == D_MODEL and batch < MOD_ROWS and ctx_len % Q_BLOCK == 0
    assert seq % TILES.row_in == 0 and (batch * seq) % TILES.row_up == 0 and seq % GRID_W == 0

    tm, tm_in, tm_up = TILES.row, TILES.row_in, TILES.row_up
    tm_ctx = ctx_len
    tm_ctx_ffn = min(tm, batch * ctx_len)
    lat_row = lambda i: i // (seq // tm)
    lat_row_in = lambda i: i // (seq // tm_in)
    ctx_row = lambda i: batch

    cond = jnp.zeros((MOD_ROWS, d), F32).at[:batch].set(c).at[batch].set(c_ctx)
    as_table = lambda t: t.reshape(MOD_ROWS * N_MOD, 1, d)
    mod = as_table(_ada(cond, w_ada, b_ada, layers=1)[0])
    tables = _rope_tables(seq)
    w_in16, w_out16 = w_in[0].astype(BF16), w_out[0].astype(BF16)

    xl = x.reshape(batch * seq, d)
    xc = ctx.reshape(batch * ctx_len, d)
    vec = lambda a, l: a[l].reshape(1, -1)
    for l in range(depth):
        g_pre, g_post = vec(g_mix_pre, l), vec(g_mix_post, l)
        g_ffn, g_ffn_out = vec(g_ffn_pre, l), vec(g_ffn_post, l)
        qn, kn = vec(q_norm, l), vec(k_norm, l)
        update_ctx = l < depth - 1

        ctx_proj = _inproj(xc, mod, ctx_row, g_pre, w_in16, qn, kn, None, tm_ctx, ctx_len, kv_only=not update_ctx)
        kv_c = ctx_proj[0]
        kv, qa, qb, z, gp = _inproj(xl, mod, lat_row_in, g_pre, w_in16, qn, kn, tables, tm_in, seq)
        oa, mod_next = _attn_a(qa, kv, kv_c, batch, seq, ctx_len,
                               ada_next=(cond, w_ada, b_ada, l + 1) if l + 1 < depth else None)
        ob = _attn_b(qb, kv, kv_c, sink[l], batch, seq, ctx_len)
        xl, h2 = _outproj(oa, ob, z, gp, conv_w[l], w_out16, xl, mod, lat_row, g_post, g_ffn, tm, seq)
        xl, w_down16, next16 = _ffn(h2, w_gate, w_up, w_down, l, xl, mod, lat_row, g_ffn_out, tm_up, tm,
                                    next_proj=(w_in, w_out) if l + 1 < depth else None)
        if update_ctx:
            _, qa_c, qb_c, z_c, gp_c = ctx_proj
            oa_c, ob_c = _attn_ctx(qa_c, qb_c, kv_c, sink[l], batch, ctx_len)
            xc, h2_c = _outproj(oa_c, ob_c, z_c, gp_c, conv_w[l], w_out16, xc, mod, ctx_row, g_post, g_ffn,
                                tm_ctx, ctx_len)
            xc, _, _ = _ffn(h2_c, w_gate, w_up, w_down, l, xc, mod, ctx_row, g_ffn_out,
                            min(tm_up, batch * ctx_len), tm_ctx_ffn, w_down16)
        if l + 1 < depth:
            w_in16, w_out16 = next16
            mod = as_table(mod_next)
    return xl.reshape(batch, seq, d)
```
